```python
import jax, jax.numpy as jnp
from jax import lax
import numpy as np

D_MODEL = 2048
BATCH = 2
SEQ = 16384
DEPTH = 1

GRID_W = 64
CTX_LEN = 256
EPS = 1e-6

A_WIDTH = D_MODEL // 2
A_HEAD = 64
A_HEADS = A_WIDTH // A_HEAD
A_DECAY_RANK = 64
A_ICLR_RANK = 64
A_LN_EPS = 64e-5

B_WIDTH = D_MODEL // 2
B_HEADS = 4
B_KEY = B_WIDTH // 2
B_DK = B_KEY // B_HEADS
B_DV = B_WIDTH // B_HEADS
B_GATE_RANK = 16
B_GATE_NORM = 16.0
B_CHUNK = 64

IN_SIZES = (3 * A_WIDTH, A_WIDTH, 2 * A_DECAY_RANK, 2 * A_ICLR_RANK,
            B_KEY, B_KEY, B_WIDTH, B_WIDTH, 2 * B_GATE_RANK, D_MODEL, D_MODEL)
P_IN = 3 * A_WIDTH + A_WIDTH + 2 * A_DECAY_RANK + 2 * A_ICLR_RANK + 2 * B_KEY + 2 * B_WIDTH + 2 * B_GATE_RANK + 2 * D_MODEL

kernel_name = "hybrid_rwkv7_gla_gated_dit_layer"


def _rmsnorm(x, g):
    xf = x.astype(jnp.float32)
    y = xf * lax.rsqrt(jnp.mean(xf * xf, axis=-1, keepdims=True) + EPS)
    return (y * g).astype(x.dtype)


def _modulation(cond, w_mod, b_mod):
    m = jax.nn.silu(cond) @ w_mod + b_mod
    return jnp.split(m, 3, axis=-1)


def _split_points():
    return [int(s) for s in np.cumsum(IN_SIZES)[:-1]]


def _grid_conv(u, w):
    bsz, t, ch = u.shape
    rows = t // GRID_W
    img = u.reshape(bsz, rows, GRID_W, ch)
    out = lax.conv_general_dilated(img, w[:, :, None, :], (1, 1), "SAME",
                                   dimension_numbers=("NHWC", "HWIO", "NHWC"),
                                   feature_group_count=ch)
    return out.reshape(bsz, t, ch)


def _seq_conv(u, w_row):
    up = jnp.pad(u, ((0, 0), (1, 1), (0, 0)))
    return up[:, :-2] * w_row[0] + up[:, 1:-1] * w_row[1] + up[:, 2:] * w_row[2]


def _features(h, lp, grid):
    f32 = jnp.float32
    bsz, t, _ = h.shape
    p = h @ lp["w_in"]
    (rkv, z_a, lora_w, lora_a, q_b, k_b, v_b, z_b, lora_g, g_a, g_b) = jnp.split(p, _split_points(), axis=-1)
    rkv = _grid_conv(rkv, lp["conv_w"]) if grid else _seq_conv(rkv, lp["conv_w"][1])
    r, k, v = jnp.split(rkv.astype(f32), 3, axis=-1)
    lw = jnp.tanh(lora_w.astype(f32).reshape(bsz, t, 2, A_DECAY_RANK))
    w_log = lp["a_w0"] + jnp.einsum("btdr,drc->btdc", lw, lp["a_w2"])
    decay = jnp.exp(-jnp.exp(-jax.nn.softplus(-w_log) - 0.5))
    la = lora_a.astype(f32).reshape(bsz, t, 2, A_ICLR_RANK)
    iclr = jax.nn.sigmoid(lp["a_a0"] + jnp.einsum("btdr,drc->btdc", la, lp["a_a2"]))
    kk = (k * lp["a_k_k"]).reshape(bsz, t, A_HEADS, A_HEAD)
    kk = kk * lax.rsqrt(jnp.sum(kk * kk, axis=-1, keepdims=True) + 1e-12)
    k_dir = k[:, :, None, :] * (1.0 + (iclr - 1.0) * lp["a_k_a"])
    heads_a = lambda z: z.reshape(z.shape[:-1] + (A_HEADS, A_HEAD))
    gk = jnp.einsum("btdr,drc->btdc", lora_g.astype(f32).reshape(bsz, t, 2, B_GATE_RANK), lp["b_gk_w2"])
    gk = jax.nn.log_sigmoid(gk + lp["b_gk_b"]) / B_GATE_NORM
    return {
        "r": heads_a(r), "v": heads_a(v), "kk": kk, "k": heads_a(k_dir),
        "decay": heads_a(decay), "iclr": heads_a(iclr), "z_a": z_a,
        "q": q_b.astype(f32).reshape(bsz, t, B_HEADS, B_DK) * (B_DK ** -0.5),
        "kg": k_b.astype(f32).reshape(bsz, t, B_HEADS, B_DK),
        "vg": v_b.astype(f32).reshape(bsz, t, B_HEADS, B_DV),
        "gk": gk.reshape(bsz, t, 2, B_HEADS, B_DK), "z_b": z_b, "g_a": g_a, "g_b": g_b,
    }


def _rwkv_scan(r, decay, k, v, a_vec, b_vec, s0, reverse, emit):
    def step(s, inp):
        w_t, k_t, v_t, a_t, b_t = inp[:5]
        sa = jnp.einsum("bhvk,bhk->bhv", s, a_t)
        s = s * w_t[:, :, None, :] + sa[..., None] * b_t[:, :, None, :] + v_t[..., None] * k_t[:, :, None, :]
        y = jnp.einsum("bhvk,bhk->bhv", s, inp[5]) if emit else None
        return s, y
    seqs = (decay, k, v, a_vec, b_vec) + ((r,) if emit else ())
    xs = tuple(jnp.moveaxis(z, 1, 0) for z in seqs)
    s, ys = lax.scan(step, s0, xs, reverse=reverse)
    return s, (jnp.moveaxis(ys, 0, 1) if emit else None)


def _gla_chunked(q, k, v, g, s0, emit):
    bsz, t, nh, dk = q.shape
    dv = v.shape[-1]
    n = t // B_CHUNK
    chunks = lambda z: z.reshape(bsz, n, B_CHUNK, nh, z.shape[-1])
    qc, kc, vc, gc = chunks(q), chunks(k), chunks(v), chunks(g)
    bcum = jnp.cumsum(gc, axis=2)
    btot = bcum[:, :, -1:]
    k_state = kc * jnp.exp(btot - bcum)
    chunk_decay = jnp.exp(btot[:, :, 0])

    def step(s, inp):
        dec, kst, vv = inp
        s_next = s * dec[..., None] + jnp.einsum("bchk,bchv->bhkv", kst, vv)
        return s_next, s
    s_fin, s_starts = lax.scan(step, s0, (jnp.moveaxis(chunk_decay, 1, 0), jnp.moveaxis(k_state, 1, 0),
                                         jnp.moveaxis(vc, 1, 0)))
    if not emit:
        return None, s_fin
    s_starts = jnp.moveaxis(s_starts, 0, 1)
    q_dec = qc * jnp.exp(bcum)
    k_inv = kc * jnp.exp(-bcum)
    inter = jnp.einsum("bnchk,bnhkv->bnchv", q_dec, s_starts)
    att = jnp.einsum("bnchk,bnshk->bnhcs", q_dec, k_inv)
    att = jnp.where(jnp.tril(jnp.ones((B_CHUNK, B_CHUNK), dtype=bool)), att, 0.0)
    intra = jnp.einsum("bnhcs,bnshv->bnchv", att, vc)
    return (inter + intra).reshape(bsz, t, nh, dv), s_fin


def _merge(f, wkv, o_b, lp):
    bsz, t = wkv.shape[:2]
    mu = jnp.mean(wkv, axis=-1, keepdims=True)
    var = jnp.mean(jnp.square(wkv - mu), axis=-1, keepdims=True)
    y_a = ((wkv - mu) * lax.rsqrt(var + A_LN_EPS)).reshape(bsz, t, A_WIDTH) * lp["a_lnx_g"] + lp["a_lnx_b"]
    bonus = jnp.einsum("bthn,btdhn,hn->bth", f["r"], f["k"], lp["a_r_k"])[..., None] * f["v"]
    y_a = (y_a + bonus.reshape(bsz, t, A_WIDTH)) * jax.nn.silu(f["z_a"])
    ob = o_b * lax.rsqrt(jnp.mean(o_b * o_b, axis=-1, keepdims=True) + EPS) * lp["b_norm_g"]
    y_b = ob.reshape(bsz, t, B_WIDTH) * jax.nn.silu(f["z_b"])
    dt = f["z_a"].dtype
    mixed = (jax.nn.sigmoid(f["g_a"]) * (y_a.astype(dt) @ lp["w_a"])
             + jax.nn.sigmoid(f["g_b"]) * (y_b.astype(dt) @ lp["w_b"]))
    return mixed @ lp["w_out"]


def _mix(f, lp, init, emit):
    sa0, sb0 = init
    a_vec = -f["kk"]
    wkv = 0.0
    s_a = []
    for d, rev in enumerate((False, True)):
        s, y = _rwkv_scan(f["r"], f["decay"][:, :, d], f["k"][:, :, d], f["v"], a_vec,
                          f["kk"] * f["iclr"][:, :, d], sa0[d], rev, emit)
        s_a.append(s)
        if emit:
            wkv = wkv + y
    o_b = 0.0
    s_b = []
    for d in range(2):
        seqs = (f["q"], f["kg"], f["vg"], f["gk"][:, :, d])
        if d == 1:
            seqs = tuple(jnp.flip(z, axis=1) for z in seqs)
        o, s = _gla_chunked(seqs[0], seqs[1], seqs[2], seqs[3], sb0[d], emit)
        s_b.append(s)
        if emit:
            o_b = o_b + (jnp.flip(o, axis=1) if d == 1 else o)
    states = ((s_a[0], s_a[1]), (s_b[0], s_b[1]))
    if not emit:
        return None, states
    return _merge(f, wkv, o_b, lp), states


def setup_inputs(seed: int = 0) -> dict:
    key = jax.random.key(seed)
    ks = jax.random.split(key, 26)
    nrm = lambda k, shape, s: jax.random.normal(k, shape, jnp.float32) * s
    L = DEPTH
    conv_id = jnp.zeros((L, 3, 3, 3 * A_WIDTH), jnp.float32).at[:, 1, 1].set(1.0)
    return {
        "x": nrm(ks[0], (BATCH, SEQ, D_MODEL), 1.0),
        "c": nrm(ks[1], (BATCH, D_MODEL), 1.0),
        "ctx": nrm(ks[2], (BATCH, CTX_LEN, D_MODEL), 1.0),
        "c_ctx": nrm(ks[3], (D_MODEL,), 1.0),
        "w_mod": nrm(ks[4], (L, D_MODEL, 3 * D_MODEL), 0.5 * D_MODEL ** -0.5),
        "b_mod": nrm(ks[5], (L, 3 * D_MODEL), 0.01),
        "norm_g": 1.0 + nrm(ks[6], (L, D_MODEL), 0.02),
        "w_in": nrm(ks[7], (L, D_MODEL, P_IN), D_MODEL ** -0.5),
        "conv_w": conv_id + nrm(ks[8], (L, 3, 3, 3 * A_WIDTH), 0.15),
        "a_w0": jax.random.uniform(ks[9], (L, 2, A_WIDTH), jnp.float32, -4.0, 2.0),
        "a_w2": nrm(ks[10], (L, 2, A_DECAY_RANK, A_WIDTH), 0.5 * A_DECAY_RANK ** -0.5),
        "a_a0": nrm(ks[11], (L, 2, A_WIDTH), 0.1),
        "a_a2": nrm(ks[12], (L, 2, A_ICLR_RANK, A_WIDTH), 0.5 * A_ICLR_RANK ** -0.5),
        "a_k_k": 0.85 + nrm(ks[13], (L, A_WIDTH), 0.05),
        "a_k_a": 1.0 + nrm(ks[14], (L, A_WIDTH), 0.05),
        "a_r_k": nrm(ks[15], (L, A_HEADS, A_HEAD), 0.1),
        "a_lnx_g": 1.0 + nrm(ks[16], (L, A_WIDTH), 0.02),
        "a_lnx_b": nrm(ks[17], (L, A_WIDTH), 0.01),
        "b_gk_w2": nrm(ks[18], (L, 2, B_GATE_RANK, B_KEY), B_GATE_RANK ** -0.5),
        "b_gk_b": nrm(ks[19], (L, 2, B_KEY), 0.5),
        "b_norm_g": 1.0 + nrm(ks[20], (L, B_DV), 0.02),
        "w_a": nrm(ks[21], (L, A_WIDTH, D_MODEL), A_WIDTH ** -0.5),
        "w_b": nrm(ks[22], (L, B_WIDTH, D_MODEL), B_WIDTH ** -0.5),
        "w_out": nrm(ks[23], (L, D_MODEL, D_MODEL), D_MODEL ** -0.5),
        "final_g": 1.0 + nrm(ks[24], (D_MODEL,), 0.02),
    }


def reference(x, c, ctx, c_ctx, w_mod, b_mod, norm_g, w_in, conv_w, a_w0, a_w2, a_a0, a_a2, a_k_k, a_k_a,
              a_r_k, a_lnx_g, a_lnx_b, b_gk_w2, b_gk_b, b_norm_g, w_a, w_b, w_out, final_g):
    bsz = x.shape[0]
    zero_a = jnp.zeros((bsz, A_HEADS, A_HEAD, A_HEAD), jnp.float32)
    zero_b = jnp.zeros((bsz, B_HEADS, B_DK, B_DV), jnp.float32)
    zero_init = ((zero_a, zero_a), (zero_b, zero_b))
    for l in range(DEPTH):
        lp = {"w_in": w_in[l], "conv_w": conv_w[l], "a_w0": a_w0[l], "a_w2": a_w2[l], "a_a0": a_a0[l],
              "a_a2": a_a2[l], "a_k_k": a_k_k[l], "a_k_a": a_k_a[l], "a_r_k": a_r_k[l],
              "a_lnx_g": a_lnx_g[l], "a_lnx_b": a_lnx_b[l], "b_gk_w2": b_gk_w2[l], "b_gk_b": b_gk_b[l],
              "b_norm_g": b_norm_g[l], "w_a": w_a[l], "w_b": w_b[l], "w_out": w_out[l]}
        last = l + 1 == DEPTH
        shift_c, scale_c, gate_c = _modulation(c_ctx, w_mod[l], b_mod[l])
        h_ctx = _rmsnorm(ctx, norm_g[l]) * (1.0 + scale_c) + shift_c
        ctx_out, ctx_states = _mix(_features(h_ctx, lp, grid=False), lp, zero_init, emit=not last)
        shift, scale, gate = _modulation(c, w_mod[l], b_mod[l])
        h = _rmsnorm(x, norm_g[l]) * (1.0 + scale[:, None]) + shift[:, None]
        y, _ = _mix(_features(h, lp, grid=True), lp, ctx_states, emit=True)
        x = x + gate[:, None] * y.astype(x.dtype)
        if not last:
            ctx = ctx + gate_c * ctx_out.astype(ctx.dtype)
    return _rmsnorm(x, final_g)
```

```python
import functools
import math

import jax
import jax.numpy as jnp
from jax import lax
from jax.experimental import pallas as pl
from jax.experimental.pallas import tpu as pltpu

F32 = jnp.float32
BF16 = jnp.bfloat16

D_MODEL = 2048
GRID_W = 64
EPS = 1e-6

A_WIDTH = 1024
A_HEAD = 64
A_HEADS = 16
A_PAIRS = A_HEADS // 2
A_RANK = 64
A_LN_EPS = 64e-5

B_WIDTH = 1024
B_HEADS = 4
B_KEY = 512
B_DK = 128
B_DV = 256
B_GATE_RANK = 16
B_GATE_NORM = 16.0

CHUNK = 64
LANES = 128

C_RKV = 0
C_ZA = 3072
C_GA = 4096
C_GB = 6144
C_QK = 8192
C_VB = 9216
C_ZB = 10240
C_LWA = 11264
C_LG = 11520
P_PAD = 11776

VMEM_LIMIT = 56 * 1024 * 1024


def _cparams(sem):
    return pltpu.CompilerParams(dimension_semantics=sem, vmem_limit_bytes=VMEM_LIMIT)


def _dot(a, b):
    return jnp.dot(a.astype(BF16), b.astype(BF16), preferred_element_type=F32)


def _dot_nt(a, b):
    return lax.dot_general(a.astype(BF16), b.astype(BF16), (((1,), (1,)), ((), ())),
                           preferred_element_type=F32)


def _split(a):
    hi = a.astype(BF16)
    lo = (a - hi.astype(F32)).astype(BF16)
    return hi, lo


def _dot3(a, b):
    ah, al = _split(a)
    bh, bl = _split(b)
    d = functools.partial(jnp.dot, preferred_element_type=F32)
    return d(ah, bh) + (d(ah, bl) + d(al, bh))


def _dot_exact_lhs(a_bf16, b):
    bh, bl = _split(b)
    d = functools.partial(jnp.dot, preferred_element_type=F32)
    return d(a_bf16, bh) + d(a_bf16, bl)


def _sigmoid(x):
    return 1.0 / (1.0 + jnp.exp(-x))


def _silu(x):
    return x * _sigmoid(x)


def _mod_kernel(c_ref, w_ref, b_ref, o_ref):
    o_ref[...] = _dot(_silu(c_ref[...]), w_ref[...]) + b_ref[...]


def _modulation(cond, w_mod, b_mod):
    rows, d = cond.shape
    n = w_mod.shape[1]
    tn = 1024
    return pl.pallas_call(
        _mod_kernel,
        grid=(n // tn,),
        in_specs=[pl.BlockSpec((rows, d), lambda j: (0, 0)),
                  pl.BlockSpec((d, tn), lambda j: (0, j)),
                  pl.BlockSpec((1, tn), lambda j: (0, j))],
        out_specs=pl.BlockSpec((rows, tn), lambda j: (0, j)),
        out_shape=jax.ShapeDtypeStruct((rows, n), F32),
        compiler_params=_cparams(("parallel",)),
    )(cond, w_mod, b_mod.reshape(1, n))


def _inproj_kernel(x_ref, sc_ref, sh_ref, g_ref, w_ref, o_ref, h_ref):
    @pl.when(pl.program_id(1) == 0)
    def _():
        x = x_ref[...]
        y = x * lax.rsqrt(jnp.mean(x * x, axis=-1, keepdims=True) + EPS)
        h = (y * g_ref[...]) * (1.0 + sc_ref[0]) + sh_ref[0]
        h_ref[...] = h.astype(BF16)

    o_ref[...] = jnp.dot(h_ref[...], w_ref[...], preferred_element_type=F32).astype(o_ref.dtype)


def _inproj(x2d, scale, shift, norm_g, w_perm, rows_per_mod, tm):
    rows, d = x2d.shape
    tn = 512
    tiles_per_mod = rows_per_mod // tm
    return pl.pallas_call(
        _inproj_kernel,
        grid=(rows // tm, P_PAD // tn),
        in_specs=[pl.BlockSpec((tm, d), lambda i, j: (i, 0)),
                  pl.BlockSpec((1, 1, d), lambda i, j: (i // tiles_per_mod, 0, 0)),
                  pl.BlockSpec((1, 1, d), lambda i, j: (i // tiles_per_mod, 0, 0)),
                  pl.BlockSpec((1, d), lambda i, j: (0, 0)),
                  pl.BlockSpec((d, tn), lambda i, j: (0, j))],
        out_specs=pl.BlockSpec((tm, tn), lambda i, j: (i, j)),
        out_shape=jax.ShapeDtypeStruct((rows, P_PAD), BF16),
        scratch_shapes=[pltpu.VMEM((tm, d), BF16)],
        compiler_params=_cparams(("parallel", "arbitrary")),
    )(x2d, scale, shift, norm_g.reshape(1, d), w_perm)


CONV_HALO = 128


def _conv_kernel(main_ref, prev_ref, next_ref, w_ref, o_ref, buf_ref, *, tiles_per_seq, on_grid):
    tt = main_ref.shape[0]
    pos = pl.program_id(0) % tiles_per_seq
    prev = prev_ref[...].astype(F32)
    nxt = next_ref[...].astype(F32)
    buf_ref[0:CONV_HALO, :] = jnp.where(pos == 0, 0.0, prev)
    buf_ref[CONV_HALO:CONV_HALO + tt, :] = main_ref[...].astype(F32)
    buf_ref[CONV_HALO + tt:, :] = jnp.where(pos == tiles_per_seq - 1, 0.0, nxt)
    col = lax.broadcasted_iota(jnp.int32, o_ref.shape, 0) % GRID_W
    acc = None
    for dc in range(3):
        part = None
        for dr in range(3):
            off = (dr - 1) * GRID_W + (dc - 1)
            term = buf_ref[CONV_HALO + off:CONV_HALO + off + tt, :] * w_ref[dr * 3 + dc:dr * 3 + dc + 1, :]
            part = term if part is None else part + term
        if on_grid and dc == 0:
            part = jnp.where(col == 0, 0.0, part)
        if on_grid and dc == 2:
            part = jnp.where(col == GRID_W - 1, 0.0, part)
        acc = part if acc is None else acc + part
    o_ref[...] = acc.astype(o_ref.dtype)


def _conv(p, w9, seq_len, on_grid, tt):
    rows = p.shape[0]
    ct = 512
    width = 3 * A_WIDTH
    tiles_per_seq = seq_len // tt
    hb = tt // CONV_HALO
    last_hb = rows // CONV_HALO - 1
    kern = functools.partial(_conv_kernel, tiles_per_seq=tiles_per_seq, on_grid=on_grid)
    return pl.pallas_call(
        kern,
        grid=(rows // tt, width // ct),
        in_specs=[pl.BlockSpec((tt, ct), lambda i, j: (i, j)),
                  pl.BlockSpec((CONV_HALO, ct), lambda i, j: (jnp.maximum(i * hb - 1, 0), j)),
                  pl.BlockSpec((CONV_HALO, ct), lambda i, j: (jnp.minimum((i + 1) * hb, last_hb), j)),
                  pl.BlockSpec((9, ct), lambda i, j: (0, j))],
        out_specs=pl.BlockSpec((tt, ct), lambda i, j: (i, j)),
        out_shape=jax.ShapeDtypeStruct((rows, width), BF16),
        scratch_shapes=[pltpu.VMEM((tt + 2 * CONV_HALO, ct), F32)],
        compiler_params=_cparams(("parallel", "parallel")),
    )(p, p, p, w9)


def _chunk_index(d, n, n_chunks):
    return n + d * (n_chunks - 1 - 2 * n)


def _rwkv_kernel(r_ref, k_ref, v_ref, lora_ref, aw2_ref, aa2_ref, aw0_ref, aa0_ref, kk_ref, ka_ref,
                 hsum_ref, hexp_ref, s0_ref, y_ref, sfin_ref, st_ref):
    d = pl.program_id(0)
    n = pl.program_id(2)

    @pl.when(n == 0)
    def _():
        st_ref[...] = s0_ref[0, 0]

    r = r_ref[...].astype(F32)
    k = k_ref[...].astype(F32)
    v = v_ref[...].astype(F32)
    lora = lora_ref[...]

    lw = jnp.tanh(lora[:, :LANES].astype(F32))
    w_log = aw0_ref[0] + _dot(lw, aw2_ref[0])
    logw = -math.exp(-0.5) * _sigmoid(w_log)
    iclr = _sigmoid(aa0_ref[0] + _dot(lora[:, LANES:], aa2_ref[0]))

    kk = k * kk_ref[...]
    ss = _dot(_dot(kk * kk, hsum_ref[...]), hexp_ref[...])
    kk = kk * lax.rsqrt(ss + 1e-12)
    kdir = k * (1.0 + (iclr - 1.0) * ka_ref[...])
    a = -kk
    b = kk * iclr

    row = lax.broadcasted_iota(jnp.int32, (CHUNK, CHUNK), 0)
    col = lax.broadcasted_iota(jnp.int32, (CHUNK, CHUNK), 1)
    sgn = 1 - 2 * d
    tri = jnp.where((row - col) * sgn >= 0, 1.0, 0.0).astype(BF16)
    cum = _dot_exact_lhs(tri, logw)
    tot = jnp.sum(logw, axis=0, keepdims=True)
    e_cum = jnp.exp(cum)
    e_neg = jnp.exp(-cum)
    e_rel = jnp.exp(tot - cum)
    a_t = a * jnp.exp(cum - logw)
    r_t = r * e_cum
    b_t = b * e_neg
    k_t = kdir * e_neg
    b_h = b * e_rel
    k_h = kdir * e_rel
    p_tot = jnp.exp(tot)

    prow = lax.broadcasted_iota(jnp.int32, (CHUNK, LANES), 0)
    pcol = lax.broadcasted_iota(jnp.int32, (CHUNK, LANES), 1) % CHUNK
    before = (prow - pcol) * sgn > 0
    upto = (prow - pcol) * sgn >= 0
    brow = lax.broadcasted_iota(jnp.int32, (LANES, LANES), 0)
    bcol = lax.broadcasted_iota(jnp.int32, (LANES, LANES), 1)
    same_head = (brow // CHUNK) == (bcol // CHUNK)
    eye = brow == bcol

    def bd(x):
        return jnp.where(same_head, jnp.concatenate([x, x], axis=0), 0.0)

    for p in range(A_PAIRS):
        sl = slice(p * LANES, (p + 1) * LANES)
        ar = jnp.concatenate([a_t[:, sl], r_t[:, sl]], axis=0)
        bk = jnp.concatenate([bd(b_t[:, sl]), bd(k_t[:, sl])], axis=0)
        mm = _dot_nt(ar, bk)
        n_bd = bd(jnp.where(before, mm[:CHUNK, :LANES], 0.0))
        mak_bd = bd(jnp.where(before, mm[:CHUNK, LANES:], 0.0))
        qb_bd = bd(jnp.where(upto, mm[CHUNK:, :LANES], 0.0))
        qk_bd = bd(jnp.where(upto, mm[CHUNK:, LANES:], 0.0))

        t_inv = jnp.where(eye, 1.0, 0.0) + n_bd
        x = _dot3(n_bd, n_bd)
        for _ in range(4):
            z = _dot3(x, jnp.concatenate([x, t_inv], axis=1))
            x = z[:, :LANES]
            t_inv = t_inv + z[:, LANES:]
        t_inv = t_inv + _dot3(x, t_inv)

        v_bd = bd(v[:, sl])
        mq = _dot(jnp.concatenate([mak_bd, qk_bd], axis=0), v_bd)
        wu = _dot3(t_inv, jnp.concatenate([bd(a_t[:, sl]), mq[:LANES]], axis=1))
        ry = _dot(qb_bd, wu) + jnp.concatenate([bd(r_t[:, sl]), mq[LANES:]], axis=1)
        lhs_t = jnp.concatenate([bd(b_h[:, sl]), bd(k_h[:, sl])], axis=0).T
        rhs = jnp.concatenate([wu, jnp.concatenate([jnp.zeros((LANES, LANES), F32), v_bd], axis=1)], axis=0)
        gh = _dot(lhs_t, rhs)
        g = gh[:, :LANES] + jnp.where(eye, p_tot[:, sl], 0.0)

        s = st_ref[p]
        rg = _dot3(jnp.concatenate([ry[:, :LANES], g], axis=0), s)
        y = rg[:LANES] + ry[:, LANES:]
        y_ref[0, 0, :, sl] = (y[:CHUNK] + y[CHUNK:]).astype(y_ref.dtype)
        st_ref[p] = rg[LANES:] + gh[:, LANES:]

    @pl.when(n == pl.num_programs(2) - 1)
    def _():
        sfin_ref[0, 0] = st_ref[...]


def _rwkv_scan(rkv_c, p, wts, s0, bsz, seq_len):
    n_chunks = seq_len // CHUNK

    def tok(cblk):
        return lambda d, b, n: (b * n_chunks + _chunk_index(d, n, n_chunks), cblk)

    per_dir3 = lambda d, b, n: (d, 0, 0)
    const2 = lambda d, b, n: (0, 0)
    state_spec = pl.BlockSpec((1, 1, A_PAIRS, LANES, LANES), lambda d, b, n: (d, b, 0, 0, 0))
    state_shape = jax.ShapeDtypeStruct((2, bsz, A_PAIRS, LANES, LANES), F32)
    return pl.pallas_call(
        _rwkv_kernel,
        grid=(2, bsz, n_chunks),
        in_specs=[pl.BlockSpec((CHUNK, A_WIDTH), tok(0)),
                  pl.BlockSpec((CHUNK, A_WIDTH), tok(1)),
                  pl.BlockSpec((CHUNK, A_WIDTH), tok(2)),
                  pl.BlockSpec((CHUNK, 2 * LANES), tok(C_LWA // (2 * LANES))),
                  pl.BlockSpec((1, LANES, A_WIDTH), per_dir3),
                  pl.BlockSpec((1, LANES, A_WIDTH), per_dir3),
                  pl.BlockSpec((1, 1, A_WIDTH), per_dir3),
                  pl.BlockSpec((1, 1, A_WIDTH), per_dir3),
                  pl.BlockSpec((1, A_WIDTH), const2),
                  pl.BlockSpec((1, A_WIDTH), const2),
                  pl.BlockSpec((A_WIDTH, LANES), const2),
                  pl.BlockSpec((LANES, A_WIDTH), const2),
                  state_spec],
        out_specs=[pl.BlockSpec((1, 1, CHUNK, A_WIDTH),
                                lambda d, b, n: (d, b, _chunk_index(d, n, n_chunks), 0)),
                   state_spec],
        out_shape=[jax.ShapeDtypeStruct((2, bsz, seq_len, A_WIDTH), F32), state_shape],
        scratch_shapes=[pltpu.VMEM((A_PAIRS, LANES, LANES), F32)],
        compiler_params=_cparams(("parallel", "parallel", "arbitrary")),
    )(rkv_c, rkv_c, rkv_c, p, wts["aw2"], wts["aa2"], wts["aw0"], wts["aa0"], wts["a_k_k"], wts["a_k_a"],
      wts["hsum_a"], wts["hexp_a"], s0)


def _gla_kernel(qk_ref, v_ref, lg_ref, w2_ref, bg_ref, s0_ref, o_ref, sfin_ref, st_ref):
    d = pl.program_id(0)
    n = pl.program_id(2)

    @pl.when(n == 0)
    def _():
        st_ref[...] = s0_ref[0, 0]

    z = _dot(lg_ref[...], w2_ref[0]) + bg_ref[0]
    gk = (jnp.minimum(z, 0.0) - jnp.log1p(jnp.exp(-jnp.abs(z)))) * (1.0 / B_GATE_NORM)

    row = lax.broadcasted_iota(jnp.int32, (CHUNK, CHUNK), 0)
    col = lax.broadcasted_iota(jnp.int32, (CHUNK, CHUNK), 1)
    upto = (row - col) * (1 - 2 * d) >= 0
    tri = jnp.where(upto, 1.0, 0.0).astype(BF16)
    cum = _dot_exact_lhs(tri, gk)
    tot = jnp.sum(gk, axis=0, keepdims=True)
    qk = qk_ref[...].astype(F32)
    q_dec = qk[:, :B_KEY] * (B_DK ** -0.5) * jnp.exp(cum)
    k_inv = qk[:, B_KEY:] * jnp.exp(-cum)
    k_state = qk[:, B_KEY:] * jnp.exp(tot - cum)
    decay = jnp.exp(tot)
    v = v_ref[...].astype(F32)

    for h in range(B_HEADS):
        ks = slice(h * B_DK, (h + 1) * B_DK)
        vs = slice(h * B_DV, (h + 1) * B_DV)
        s_t = st_ref[h]
        att = jnp.where(upto, _dot_nt(q_dec[:, ks], k_inv[:, ks]), 0.0)
        o = _dot_nt(q_dec[:, ks], s_t) + _dot(att, v[:, vs])
        o_ref[0, 0, :, vs] = o.astype(o_ref.dtype)
        st_ref[h] = s_t * decay[:, ks] + _dot(v[:, vs].T, k_state[:, ks])

    @pl.when(n == pl.num_programs(2) - 1)
    def _():
        sfin_ref[0, 0] = st_ref[...]


def _gla_scan(p, wts, s0, bsz, seq_len):
    n_chunks = seq_len // CHUNK

    def tok(cblk):
        return lambda d, b, n: (b * n_chunks + _chunk_index(d, n, n_chunks), cblk)

    per_dir3 = lambda d, b, n: (d, 0, 0)
    state_spec = pl.BlockSpec((1, 1, B_HEADS, B_DV, B_DK), lambda d, b, n: (d, b, 0, 0, 0))
    state_shape = jax.ShapeDtypeStruct((2, bsz, B_HEADS, B_DV, B_DK), F32)
    return pl.pallas_call(
        _gla_kernel,
        grid=(2, bsz, n_chunks),
        in_specs=[pl.BlockSpec((CHUNK, 2 * B_KEY), tok(C_QK // (2 * B_KEY))),
                  pl.BlockSpec((CHUNK, B_WIDTH), tok(C_VB // B_WIDTH)),
                  pl.BlockSpec((CHUNK, LANES), tok(C_LG // LANES)),
                  pl.BlockSpec((1, LANES, B_KEY), per_dir3),
                  pl.BlockSpec((1, 1, B_KEY), per_dir3),
                  state_spec],
        out_specs=[pl.BlockSpec((1, 1, CHUNK, B_WIDTH),
                                lambda d, b, n: (d, b, _chunk_index(d, n, n_chunks), 0)),
                   state_spec],
        out_shape=[jax.ShapeDtypeStruct((2, bsz, seq_len, B_WIDTH), F32), state_shape],
        scratch_shapes=[pltpu.VMEM((B_HEADS, B_DV, B_DK), F32)],
        compiler_params=_cparams(("parallel", "parallel", "arbitrary")),
    )(p, p, p, wts["gw2"], wts["gb"], s0)


def _merge_kernel(x_ref, gate_ref, ya_ref, ob_ref, r_ref, k_ref, v_ref, la_ref, za_ref, zb_ref, ga_ref, gb_ref,
                  aa2_ref, aa0_ref, ka_ref, rk_ref, lng_ref, lnb_ref, bng_ref, hsa_ref, hea_ref, hsb_ref,
                  heb_ref, wa_ref, wb_ref, wo_ref, fg_ref, o_ref):
    wkv = ya_ref[0, 0] + ya_ref[1, 0]
    mu = _dot(_dot(wkv, hsa_ref[...]), hea_ref[...]) * (1.0 / A_HEAD)
    cen = wkv - mu
    var = _dot(_dot(cen * cen, hsa_ref[...]), hea_ref[...]) * (1.0 / A_HEAD)
    y_a = cen * lax.rsqrt(var + A_LN_EPS) * lng_ref[...] + lnb_ref[...]
    r = r_ref[...].astype(F32)
    k = k_ref[...].astype(F32)
    v = v_ref[...].astype(F32)
    la = la_ref[...][:, LANES:]
    iclr_sum = (_sigmoid(aa0_ref[0] + _dot(la, aa2_ref[0])) + _sigmoid(aa0_ref[1] + _dot(la, aa2_ref[1])))
    rk = r * k * rk_ref[...] * (2.0 + (iclr_sum - 2.0) * ka_ref[...])
    bonus = _dot(_dot(rk, hsa_ref[...]), hea_ref[...])
    y_a = (y_a + bonus * v) * _silu(za_ref[...].astype(F32))
    ob = ob_ref[0, 0] + ob_ref[1, 0]
    ms = _dot(_dot(ob * ob, hsb_ref[...]), heb_ref[...]) * (1.0 / B_DV)
    y_b = ob * lax.rsqrt(ms + EPS) * bng_ref[...] * _silu(zb_ref[...].astype(F32))
    mixed = (_sigmoid(ga_ref[...].astype(F32)) * _dot(y_a, wa_ref[...])
             + _sigmoid(gb_ref[...].astype(F32)) * _dot(y_b, wb_ref[...]))
    y = _dot(mixed, wo_ref[...])
    xo = x_ref[...] + gate_ref[0] * y
    o_ref[...] = xo * lax.rsqrt(jnp.mean(xo * xo, axis=-1, keepdims=True) + EPS) * fg_ref[...]


def _merge(x2d, gate, y_a, o_b, rkv_c, p, wts, bsz, seq_len, tm):
    rows, d = x2d.shape
    tiles_per_seq = seq_len // tm
    tokc = lambda cblk: (lambda i: (i, cblk))
    scan_spec = lambda w: pl.BlockSpec((2, 1, tm, w), lambda i: (0, i // tiles_per_seq, i % tiles_per_seq, 0))
    const = lambda shape: pl.BlockSpec(shape, lambda i: (0,) * len(shape), pipeline_mode=pl.Buffered(1))
    return pl.pallas_call(
        _merge_kernel,
        grid=(rows // tm,),
        in_specs=[pl.BlockSpec((tm, d), lambda i: (i, 0)),
                  pl.BlockSpec((1, 1, d), lambda i: (i // tiles_per_seq, 0, 0)),
                  scan_spec(A_WIDTH), scan_spec(B_WIDTH),
                  pl.BlockSpec((tm, A_WIDTH), tokc(0)),
                  pl.BlockSpec((tm, A_WIDTH), tokc(1)),
                  pl.BlockSpec((tm, A_WIDTH), tokc(2)),
                  pl.BlockSpec((tm, 2 * LANES), tokc(C_LWA // (2 * LANES))),
                  pl.BlockSpec((tm, A_WIDTH), tokc(C_ZA // A_WIDTH)),
                  pl.BlockSpec((tm, B_WIDTH), tokc(C_ZB // B_WIDTH)),
                  pl.BlockSpec((tm, d), tokc(C_GA // d)),
                  pl.BlockSpec((tm, d), tokc(C_GB // d)),
                  const((2, LANES, A_WIDTH)), const((2, 1, A_WIDTH)), const((1, A_WIDTH)), const((1, A_WIDTH)),
                  const((1, A_WIDTH)), const((1, A_WIDTH)), const((1, B_WIDTH)),
                  const((A_WIDTH, LANES)), const((LANES, A_WIDTH)), const((B_WIDTH, LANES)),
                  const((LANES, B_WIDTH)),
                  const((A_WIDTH, d)), const((B_WIDTH, d)), const((d, d)), const((1, d))],
        out_specs=pl.BlockSpec((tm, d), lambda i: (i, 0)),
        out_shape=jax.ShapeDtypeStruct((rows, d), F32),
        compiler_params=_cparams(("parallel",)),
    )(x2d, gate, y_a, o_b, rkv_c, rkv_c, rkv_c, p, p, p, p, p,
      wts["aa2"], wts["aa0"], wts["a_k_a"], wts["a_r_k"], wts["lnx_g"], wts["lnx_b"], wts["bng"],
      wts["hsum_a"], wts["hexp_a"], wts["hsum_b"], wts["hexp_b"], wts["w_a"], wts["w_b"], wts["w_out"],
      wts["final_g"])


def _head_indicator(width, head):
    m = (jnp.arange(width)[:, None] // head == jnp.arange(LANES)[None, :]).astype(BF16)
    return m, m.T


def _prep_weights(w_in, conv_w, a_w0, a_w2, a_a0, a_a2, a_k_k, a_k_a, a_r_k, a_lnx_g, a_lnx_b, b_gk_w2,
                  b_gk_b, b_norm_g, w_a, w_b, w_out, final_g):
    sizes = (3 * A_WIDTH, A_WIDTH, 2 * A_RANK, 2 * A_RANK, B_KEY, B_KEY, B_WIDTH, B_WIDTH, 2 * B_GATE_RANK,
             D_MODEL, D_MODEL)
    offs = [0]
    for s in sizes:
        offs.append(offs[-1] + s)
    seg = lambda i: w_in[:, offs[i]:offs[i + 1]]
    rkv, z_a, lora_w, lora_a, q_b, k_b, v_b, z_b, lora_g, g_a, g_b = (seg(i) for i in range(11))
    pad_g = jnp.zeros((D_MODEL, LANES - 2 * B_GATE_RANK), w_in.dtype)
    pad_end = jnp.zeros((D_MODEL, P_PAD - C_LG - LANES), w_in.dtype)
    w_perm = jnp.concatenate([rkv, z_a, g_a, g_b, q_b, k_b, v_b, z_b, lora_w, lora_a, lora_g, pad_g, pad_end],
                             axis=1).astype(BF16)

    def dir_pad(w2, rank):
        out = jnp.zeros((2, LANES, w2.shape[-1]), F32)
        for d in range(2):
            out = out.at[d, d * rank:(d + 1) * rank].set(w2[d])
        return out.astype(BF16)

    hsum_a, hexp_a = _head_indicator(A_WIDTH, A_HEAD)
    hsum_b, hexp_b = _head_indicator(B_WIDTH, B_DV)
    return {
        "w_perm": w_perm,
        "conv9": conv_w.reshape(9, 3 * A_WIDTH),
        "aw2": dir_pad(a_w2, A_RANK), "aa2": dir_pad(a_a2, A_RANK),
        "aw0": a_w0.reshape(2, 1, A_WIDTH), "aa0": a_a0.reshape(2, 1, A_WIDTH),
        "a_k_k": a_k_k.reshape(1, A_WIDTH), "a_k_a": a_k_a.reshape(1, A_WIDTH),
        "a_r_k": a_r_k.reshape(1, A_WIDTH),
        "lnx_g": a_lnx_g.reshape(1, A_WIDTH), "lnx_b": a_lnx_b.reshape(1, A_WIDTH),
        "gw2": dir_pad(b_gk_w2, B_GATE_RANK), "gb": b_gk_b.reshape(2, 1, B_KEY),
        "bng": jnp.tile(b_norm_g, B_HEADS).reshape(1, B_WIDTH),
        "hsum_a": hsum_a, "hexp_a": hexp_a, "hsum_b": hsum_b, "hexp_b": hexp_b,
        "w_a": w_a.astype(BF16), "w_b": w_b.astype(BF16), "w_out": w_out.astype(BF16),
        "final_g": final_g.reshape(1, D_MODEL),
    }


def _pick_tile(n, cap):
    t = cap
    while n % t:
        t //= 2
    return t


def kernel(x, c, ctx, c_ctx, w_mod, b_mod, norm_g, w_in, conv_w, a_w0, a_w2, a_a0, a_a2, a_k_k, a_k_a,
           a_r_k, a_lnx_g, a_lnx_b, b_gk_w2, b_gk_b, b_norm_g, w_a, w_b, w_out, final_g):
    assert w_in.shape[0] == 1, "single layer"
    bsz, seq_len, d = x.shape
    ctx_len = ctx.shape[1]
    wts = _prep_weights(w_in[0], conv_w[0], a_w0[0], a_w2[0], a_a0[0], a_a2[0], a_k_k[0], a_k_a[0], a_r_k[0],
                        a_lnx_g[0], a_lnx_b[0], b_gk_w2[0], b_gk_b[0], b_norm_g[0], w_a[0], w_b[0], w_out[0],
                        final_g)

    cond = jnp.concatenate([c, c_ctx[None], jnp.zeros((8 - bsz - 1, d), F32)], axis=0)
    mod = _modulation(cond, w_mod[0], b_mod[0])
    shift, scale, gate = mod[:, :d], mod[:, d:2 * d], mod[:, 2 * d:]
    as_rows = lambda m, lo, hi: m[lo:hi].reshape(hi - lo, 1, d)

    ctx2d = ctx.reshape(bsz * ctx_len, d)
    p_ctx = _inproj(ctx2d, as_rows(scale, bsz, bsz + 1), as_rows(shift, bsz, bsz + 1), norm_g[0],
                    wts["w_perm"], bsz * ctx_len, _pick_tile(bsz * ctx_len, 1024))
    conv_ctx = wts["conv9"] * jnp.array([0.0, 1.0, 0.0], F32).repeat(3)[:, None]
    rkv_ctx = _conv(p_ctx, conv_ctx, ctx_len, False, _pick_tile(ctx_len, 512))
    zero_a = jnp.zeros((2, bsz, A_PAIRS, LANES, LANES), F32)
    zero_b = jnp.zeros((2, bsz, B_HEADS, B_DV, B_DK), F32)
    _, sa_ctx = _rwkv_scan(rkv_ctx, p_ctx, wts, zero_a, bsz, ctx_len)
    _, sb_ctx = _gla_scan(p_ctx, wts, zero_b, bsz, ctx_len)

    x2d = x.reshape(bsz * seq_len, d)
    p = _inproj(x2d, as_rows(scale, 0, bsz), as_rows(shift, 0, bsz), norm_g[0], wts["w_perm"], seq_len,
                _pick_tile(seq_len, 1024))
    rkv_c = _conv(p, wts["conv9"], seq_len, True, _pick_tile(seq_len, 512))
    y_a, _ = _rwkv_scan(rkv_c, p, wts, sa_ctx, bsz, seq_len)
    o_b, _ = _gla_scan(p, wts, sb_ctx, bsz, seq_len)
    out = _merge(x2d, as_rows(gate, 0, bsz), y_a, o_b, rkv_c, p, wts, bsz, seq_len, _pick_tile(seq_len, 256))
    return out.reshape(bsz, seq_len, d)
```

```python
import functools
import math

import jax
import jax.numpy as jnp
from jax import lax
from jax.experimental import pallas as pl
from jax.experimental.pallas import tpu as pltpu

F32 = jnp.float32
BF16 = jnp.bfloat16

D_MODEL = 2048
GRID_W = 64
EPS = 1e-6

A_WIDTH = 1024
A_HEAD = 64
A_HEADS = 16
A_PAIRS = A_HEADS // 2
A_RANK = 64
A_LN_EPS = 64e-5

B_WIDTH = 1024
B_HEADS = 4
B_KEY = 512
B_DK = 128
B_DV = 256
B_GATE_RANK = 16
B_GATE_NORM = 16.0

CHUNK = 64
PAIR_GROUP = 8
LANES = 128

C_RKV = 0
C_ZA = 3072
C_GA = 4096
C_GB = 6144
C_QK = 8192
C_VB = 9216
C_ZB = 10240
C_LWA = 11264
C_LG = 11520
P_PAD = 11776

VMEM_LIMIT = 56 * 1024 * 1024


def _cparams(sem):
    return pltpu.CompilerParams(dimension_semantics=sem, vmem_limit_bytes=VMEM_LIMIT)


def _dot(a, b):
    return jnp.dot(a.astype(BF16), b.astype(BF16), preferred_element_type=F32)


def _dot_nt(a, b):
    return lax.dot_general(a.astype(BF16), b.astype(BF16), (((1,), (1,)), ((), ())),
                           preferred_element_type=F32)


def _split(a):
    hi = a.astype(BF16)
    lo = (a - hi.astype(F32)).astype(BF16)
    return hi, lo


def _dot_exact_lhs(a_bf16, b):
    bh, bl = _split(b)
    d = functools.partial(jnp.dot, preferred_element_type=F32)
    return d(a_bf16, bh) + d(a_bf16, bl)


def _sigmoid(x):
    return 1.0 / (1.0 + jnp.exp(-x))


def _silu(x):
    return x * _sigmoid(x)


def _mod_kernel(c_ref, w_ref, b_ref, o_ref):
    o_ref[...] = _dot(_silu(c_ref[...]), w_ref[...]) + b_ref[...]


def _modulation(cond, w_mod, b_mod):
    rows, d = cond.shape
    n = w_mod.shape[1]
    tn = 1024
    return pl.pallas_call(
        _mod_kernel,
        grid=(n // tn,),
        in_specs=[pl.BlockSpec((rows, d), lambda j: (0, 0)),
                  pl.BlockSpec((d, tn), lambda j: (0, j)),
                  pl.BlockSpec((1, tn), lambda j: (0, j))],
        out_specs=pl.BlockSpec((rows, tn), lambda j: (0, j)),
        out_shape=jax.ShapeDtypeStruct((rows, n), F32),
        compiler_params=_cparams(("parallel",)),
    )(cond, w_mod, b_mod.reshape(1, n))


def _inproj_kernel(x_ref, sc_ref, sh_ref, g_ref, w_ref, o_ref, h_ref):
    @pl.when(pl.program_id(1) == 0)
    def _():
        x = x_ref[...]
        y = x * lax.rsqrt(jnp.mean(x * x, axis=-1, keepdims=True) + EPS)
        h = (y * g_ref[...]) * (1.0 + sc_ref[0]) + sh_ref[0]
        h_ref[...] = h.astype(BF16)

    o_ref[...] = jnp.dot(h_ref[...], w_ref[...], preferred_element_type=F32).astype(o_ref.dtype)


def _inproj(x2d, scale, shift, norm_g, w_perm, rows_per_mod, tm):
    rows, d = x2d.shape
    tn = 512
    tiles_per_mod = rows_per_mod // tm
    return pl.pallas_call(
        _inproj_kernel,
        grid=(rows // tm, P_PAD // tn),
        in_specs=[pl.BlockSpec((tm, d), lambda i, j: (i, 0)),
                  pl.BlockSpec((1, 1, d), lambda i, j: (i // tiles_per_mod, 0, 0)),
                  pl.BlockSpec((1, 1, d), lambda i, j: (i // tiles_per_mod, 0, 0)),
                  pl.BlockSpec((1, d), lambda i, j: (0, 0)),
                  pl.BlockSpec((d, tn), lambda i, j: (0, j))],
        out_specs=pl.BlockSpec((tm, tn), lambda i, j: (i, j)),
        out_shape=jax.ShapeDtypeStruct((rows, P_PAD), BF16),
        scratch_shapes=[pltpu.VMEM((tm, d), BF16)],
        compiler_params=_cparams(("parallel", "arbitrary")),
    )(x2d, scale, shift, norm_g.reshape(1, d), w_perm)


CONV_HALO = 128


def _conv_kernel(main_ref, prev_ref, next_ref, w_ref, o_ref, buf_ref, *, tiles_per_seq, on_grid):
    tt = main_ref.shape[0]
    pos = pl.program_id(0) % tiles_per_seq
    prev = prev_ref[...].astype(F32)
    nxt = next_ref[...].astype(F32)
    buf_ref[0:CONV_HALO, :] = jnp.where(pos == 0, 0.0, prev)
    buf_ref[CONV_HALO:CONV_HALO + tt, :] = main_ref[...].astype(F32)
    buf_ref[CONV_HALO + tt:, :] = jnp.where(pos == tiles_per_seq - 1, 0.0, nxt)
    col = lax.broadcasted_iota(jnp.int32, o_ref.shape, 0) % GRID_W
    acc = None
    for dc in range(3):
        part = None
        for dr in range(3):
            off = (dr - 1) * GRID_W + (dc - 1)
            term = buf_ref[CONV_HALO + off:CONV_HALO + off + tt, :] * w_ref[dr * 3 + dc:dr * 3 + dc + 1, :]
            part = term if part is None else part + term
        if on_grid and dc == 0:
            part = jnp.where(col == 0, 0.0, part)
        if on_grid and dc == 2:
            part = jnp.where(col == GRID_W - 1, 0.0, part)
        acc = part if acc is None else acc + part
    o_ref[...] = acc.astype(o_ref.dtype)


def _conv(p, w9, seq_len, on_grid, tt):
    rows = p.shape[0]
    ct = 512
    width = 3 * A_WIDTH
    tiles_per_seq = seq_len // tt
    hb = tt // CONV_HALO
    last_hb = rows // CONV_HALO - 1
    kern = functools.partial(_conv_kernel, tiles_per_seq=tiles_per_seq, on_grid=on_grid)
    return pl.pallas_call(
        kern,
        grid=(rows // tt, width // ct),
        in_specs=[pl.BlockSpec((tt, ct), lambda i, j: (i, j)),
                  pl.BlockSpec((CONV_HALO, ct), lambda i, j: (jnp.maximum(i * hb - 1, 0), j)),
                  pl.BlockSpec((CONV_HALO, ct), lambda i, j: (jnp.minimum((i + 1) * hb, last_hb), j)),
                  pl.BlockSpec((9, ct), lambda i, j: (0, j))],
        out_specs=pl.BlockSpec((tt, ct), lambda i, j: (i, j)),
        out_shape=jax.ShapeDtypeStruct((rows, width), BF16),
        scratch_shapes=[pltpu.VMEM((tt + 2 * CONV_HALO, ct), F32)],
        compiler_params=_cparams(("parallel", "parallel")),
    )(p, p, p, w9)


def _chunk_index(d, n, n_chunks):
    return n + d * (n_chunks - 1 - 2 * n)


def _rwkv_kernel(r_ref, k_ref, v_ref, lora_ref, aw2_ref, aa2_ref, aw0_ref, aa0_ref, kk_ref, ka_ref,
                 hsum_ref, hexp_ref, s0_ref, y_ref, sfin_ref, st_ref):
    d = pl.program_id(0)
    n = pl.program_id(2)

    @pl.when(n == 0)
    def _():
        st_ref[...] = s0_ref[0, 0]

    r = r_ref[...].astype(F32)
    k = k_ref[...].astype(F32)
    v = v_ref[...].astype(F32)
    lora = lora_ref[...]

    lw = jnp.tanh(lora[:, :LANES].astype(F32))
    w_log = aw0_ref[0] + _dot(lw, aw2_ref[0])
    logw = -math.exp(-0.5) * _sigmoid(w_log)
    iclr = _sigmoid(aa0_ref[0] + _dot(lora[:, LANES:], aa2_ref[0]))

    kk = k * kk_ref[...]
    ss = _dot(_dot(kk * kk, hsum_ref[...]), hexp_ref[...])
    kk = kk * lax.rsqrt(ss + 1e-12)
    kdir = k * (1.0 + (iclr - 1.0) * ka_ref[...])
    a = -kk
    b = kk * iclr

    row = lax.broadcasted_iota(jnp.int32, (CHUNK, CHUNK), 0)
    col = lax.broadcasted_iota(jnp.int32, (CHUNK, CHUNK), 1)
    sgn = 1 - 2 * d
    tri = jnp.where((row - col) * sgn >= 0, 1.0, 0.0).astype(BF16)
    cum = _dot_exact_lhs(tri, logw)
    tot = jnp.sum(logw, axis=0, keepdims=True)
    e_cum = jnp.exp(cum)
    e_neg = jnp.exp(-cum)
    e_rel = jnp.exp(tot - cum)
    a_t = a * jnp.exp(cum - logw)
    r_t = r * e_cum
    b_t = b * e_neg
    k_t = kdir * e_neg
    b_h = b * e_rel
    k_h = kdir * e_rel
    p_tot = jnp.exp(tot)

    prow = lax.broadcasted_iota(jnp.int32, (CHUNK, LANES), 0)
    pcol = lax.broadcasted_iota(jnp.int32, (CHUNK, LANES), 1) % CHUNK
    before = (prow - pcol) * sgn > 0
    upto = (prow - pcol) * sgn >= 0
    brow = lax.broadcasted_iota(jnp.int32, (LANES, LANES), 0)
    bcol = lax.broadcasted_iota(jnp.int32, (LANES, LANES), 1)
    same_head = (brow // CHUNK) == (bcol // CHUNK)
    eye = brow == bcol

    def bd(x):
        return jnp.where(same_head, jnp.concatenate([x, x], axis=0), 0.0)

    for g0 in range(0, A_PAIRS, PAIR_GROUP):
        grp = range(g0, g0 + PAIR_GROUP)
        sls = [slice(p * LANES, (p + 1) * LANES) for p in grp]
        each = lambda f, *lists: [f(*xs) for xs in zip(*lists)]
        mm = [_dot_nt(jnp.concatenate([a_t[:, sl], r_t[:, sl]], axis=0),
                      jnp.concatenate([bd(b_t[:, sl]), bd(k_t[:, sl])], axis=0)) for sl in sls]
        n_bd = [bd(jnp.where(before, m[:CHUNK, :LANES], 0.0)) for m in mm]
        mak_bd = [bd(jnp.where(before, m[:CHUNK, LANES:], 0.0)) for m in mm]
        qb_bd = [bd(jnp.where(upto, m[CHUNK:, :LANES], 0.0)) for m in mm]
        qk_bd = [bd(jnp.where(upto, m[CHUNK:, LANES:], 0.0)) for m in mm]
        v_bd = [bd(v[:, sl]) for sl in sls]
        mq = each(lambda mk, qk, vb: _dot(jnp.concatenate([mk, qk], axis=0), vb), mak_bd, qk_bd, v_bd)

        t_inv = [jnp.where(eye, 1.0, 0.0) + nb for nb in n_bd]
        x = [_dot(nb, nb) for nb in n_bd]
        for _ in range(4):
            z = each(lambda xx, tt: _dot(xx, jnp.concatenate([xx, tt], axis=1)), x, t_inv)
            x = [zz[:, :LANES] for zz in z]
            t_inv = each(lambda tt, zz: tt + zz[:, LANES:], t_inv, z)
        t_inv = each(lambda tt, xx: tt + _dot(xx, tt), t_inv, x)

        wu = each(lambda tt, sl, m: _dot(tt, jnp.concatenate([bd(a_t[:, sl]), m[:LANES]], axis=1)),
                  t_inv, sls, mq)
        ry = each(lambda qb, w, sl, m: _dot(qb, w) + jnp.concatenate([bd(r_t[:, sl]), m[LANES:]], axis=1),
                  qb_bd, wu, sls, mq)
        zeros = jnp.zeros((LANES, LANES), F32)
        gh = each(lambda sl, w, vb: _dot(jnp.concatenate([bd(b_h[:, sl]), bd(k_h[:, sl])], axis=0).T,
                                         jnp.concatenate([w, jnp.concatenate([zeros, vb], axis=1)], axis=0)),
                  sls, wu, v_bd)
        for p, sl, ry_p, gh_p in zip(grp, sls, ry, gh):
            g = gh_p[:, :LANES] + jnp.where(eye, p_tot[:, sl], 0.0)
            rg = _dot(jnp.concatenate([ry_p[:, :LANES], g], axis=0), st_ref[p])
            y = rg[:LANES] + ry_p[:, LANES:]
            y_ref[0, 0, :, sl] = (y[:CHUNK] + y[CHUNK:]).astype(y_ref.dtype)
            st_ref[p] = rg[LANES:] + gh_p[:, LANES:]

    @pl.when(n == pl.num_programs(2) - 1)
    def _():
        sfin_ref[0, 0] = st_ref[...]


def _rwkv_scan(rkv_c, p, wts, s0, bsz, seq_len):
    n_chunks = seq_len // CHUNK

    def tok(cblk):
        return lambda d, b, n: (b * n_chunks + _chunk_index(d, n, n_chunks), cblk)

    per_dir3 = lambda d, b, n: (d, 0, 0)
    const2 = lambda d, b, n: (0, 0)
    state_spec = pl.BlockSpec((1, 1, A_PAIRS, LANES, LANES), lambda d, b, n: (d, b, 0, 0, 0))
    state_shape = jax.ShapeDtypeStruct((2, bsz, A_PAIRS, LANES, LANES), F32)
    return pl.pallas_call(
        _rwkv_kernel,
        grid=(2, bsz, n_chunks),
        in_specs=[pl.BlockSpec((CHUNK, A_WIDTH), tok(0)),
                  pl.BlockSpec((CHUNK, A_WIDTH), tok(1)),
                  pl.BlockSpec((CHUNK, A_WIDTH), tok(2)),
                  pl.BlockSpec((CHUNK, 2 * LANES), tok(C_LWA // (2 * LANES))),
                  pl.BlockSpec((1, LANES, A_WIDTH), per_dir3),
                  pl.BlockSpec((1, LANES, A_WIDTH), per_dir3),
                  pl.BlockSpec((1, 1, A_WIDTH), per_dir3),
                  pl.BlockSpec((1, 1, A_WIDTH), per_dir3),
                  pl.BlockSpec((1, A_WIDTH), const2),
                  pl.BlockSpec((1, A_WIDTH), const2),
                  pl.BlockSpec((A_WIDTH, LANES), const2),
                  pl.BlockSpec((LANES, A_WIDTH), const2),
                  state_spec],
        out_specs=[pl.BlockSpec((1, 1, CHUNK, A_WIDTH),
                                lambda d, b, n: (d, b, _chunk_index(d, n, n_chunks), 0)),
                   state_spec],
        out_shape=[jax.ShapeDtypeStruct((2, bsz, seq_len, A_WIDTH), F32), state_shape],
        scratch_shapes=[pltpu.VMEM((A_PAIRS, LANES, LANES), F32)],
        compiler_params=_cparams(("parallel", "parallel", "arbitrary")),
    )(rkv_c, rkv_c, rkv_c, p, wts["aw2"], wts["aa2"], wts["aw0"], wts["aa0"], wts["a_k_k"], wts["a_k_a"],
      wts["hsum_a"], wts["hexp_a"], s0)


def _gla_kernel(qk_ref, v_ref, lg_ref, w2_ref, bg_ref, s0_ref, o_ref, sfin_ref, st_ref):
    d = pl.program_id(0)
    n = pl.program_id(2)

    @pl.when(n == 0)
    def _():
        st_ref[...] = s0_ref[0, 0]

    z = _dot(lg_ref[...], w2_ref[0]) + bg_ref[0]
    gk = (jnp.minimum(z, 0.0) - jnp.log1p(jnp.exp(-jnp.abs(z)))) * (1.0 / B_GATE_NORM)

    row = lax.broadcasted_iota(jnp.int32, (CHUNK, CHUNK), 0)
    col = lax.broadcasted_iota(jnp.int32, (CHUNK, CHUNK), 1)
    upto = (row - col) * (1 - 2 * d) >= 0
    tri = jnp.where(upto, 1.0, 0.0).astype(BF16)
    cum = _dot_exact_lhs(tri, gk)
    tot = jnp.sum(gk, axis=0, keepdims=True)
    qk = qk_ref[...].astype(F32)
    q_dec = qk[:, :B_KEY] * (B_DK ** -0.5) * jnp.exp(cum)
    k_inv = qk[:, B_KEY:] * jnp.exp(-cum)
    k_state = qk[:, B_KEY:] * jnp.exp(tot - cum)
    decay = jnp.exp(tot)
    v = v_ref[...].astype(F32)

    for h in range(B_HEADS):
        ks = slice(h * B_DK, (h + 1) * B_DK)
        vs = slice(h * B_DV, (h + 1) * B_DV)
        s_t = st_ref[h]
        att = jnp.where(upto, _dot_nt(q_dec[:, ks], k_inv[:, ks]), 0.0)
        o = _dot_nt(q_dec[:, ks], s_t) + _dot(att, v[:, vs])
        o_ref[0, 0, :, vs] = o.astype(o_ref.dtype)
        st_ref[h] = s_t * decay[:, ks] + _dot(v[:, vs].T, k_state[:, ks])

    @pl.when(n == pl.num_programs(2) - 1)
    def _():
        sfin_ref[0, 0] = st_ref[...]


def _gla_scan(p, wts, s0, bsz, seq_len):
    n_chunks = seq_len // CHUNK

    def tok(cblk):
        return lambda d, b, n: (b * n_chunks + _chunk_index(d, n, n_chunks), cblk)

    per_dir3 = lambda d, b, n: (d, 0, 0)
    state_spec = pl.BlockSpec((1, 1, B_HEADS, B_DV, B_DK), lambda d, b, n: (d, b, 0, 0, 0))
    state_shape = jax.ShapeDtypeStruct((2, bsz, B_HEADS, B_DV, B_DK), F32)
    return pl.pallas_call(
        _gla_kernel,
        grid=(2, bsz, n_chunks),
        in_specs=[pl.BlockSpec((CHUNK, 2 * B_KEY), tok(C_QK // (2 * B_KEY))),
                  pl.BlockSpec((CHUNK, B_WIDTH), tok(C_VB // B_WIDTH)),
                  pl.BlockSpec((CHUNK, LANES), tok(C_LG // LANES)),
                  pl.BlockSpec((1, LANES, B_KEY), per_dir3),
                  pl.BlockSpec((1, 1, B_KEY), per_dir3),
                  state_spec],
        out_specs=[pl.BlockSpec((1, 1, CHUNK, B_WIDTH),
                                lambda d, b, n: (d, b, _chunk_index(d, n, n_chunks), 0)),
                   state_spec],
        out_shape=[jax.ShapeDtypeStruct((2, bsz, seq_len, B_WIDTH), F32), state_shape],
        scratch_shapes=[pltpu.VMEM((B_HEADS, B_DV, B_DK), F32)],
        compiler_params=_cparams(("parallel", "parallel", "arbitrary")),
    )(p, p, p, wts["gw2"], wts["gb"], s0)


def _merge_kernel(x_ref, gate_ref, ya_ref, ob_ref, r_ref, k_ref, v_ref, la_ref, za_ref, zb_ref, ga_ref, gb_ref,
                  aa2_ref, aa0_ref, ka_ref, rk_ref, lng_ref, lnb_ref, bng_ref, hsa_ref, hea_ref, hsb_ref,
                  heb_ref, wa_ref, wb_ref, wo_ref, fg_ref, o_ref):
    wkv = ya_ref[0, 0] + ya_ref[1, 0]
    mu = _dot(_dot(wkv, hsa_ref[...]), hea_ref[...]) * (1.0 / A_HEAD)
    cen = wkv - mu
    var = _dot(_dot(cen * cen, hsa_ref[...]), hea_ref[...]) * (1.0 / A_HEAD)
    y_a = cen * lax.rsqrt(var + A_LN_EPS) * lng_ref[...] + lnb_ref[...]
    r = r_ref[...].astype(F32)
    k = k_ref[...].astype(F32)
    v = v_ref[...].astype(F32)
    la = la_ref[...][:, LANES:]
    iclr_sum = (_sigmoid(aa0_ref[0] + _dot(la, aa2_ref[0])) + _sigmoid(aa0_ref[1] + _dot(la, aa2_ref[1])))
    rk = r * k * rk_ref[...] * (2.0 + (iclr_sum - 2.0) * ka_ref[...])
    bonus = _dot(_dot(rk, hsa_ref[...]), hea_ref[...])
    y_a = (y_a + bonus * v) * _silu(za_ref[...].astype(F32))
    ob = ob_ref[0, 0] + ob_ref[1, 0]
    ms = _dot(_dot(ob * ob, hsb_ref[...]), heb_ref[...]) * (1.0 / B_DV)
    y_b = ob * lax.rsqrt(ms + EPS) * bng_ref[...] * _silu(zb_ref[...].astype(F32))
    mixed = (_sigmoid(ga_ref[...].astype(F32)) * _dot(y_a, wa_ref[...])
             + _sigmoid(gb_ref[...].astype(F32)) * _dot(y_b, wb_ref[...]))
    y = _dot(mixed, wo_ref[...])
    xo = x_ref[...] + gate_ref[0] * y
    o_ref[...] = xo * lax.rsqrt(jnp.mean(xo * xo, axis=-1, keepdims=True) + EPS) * fg_ref[...]


def _merge(x2d, gate, y_a, o_b, rkv_c, p, wts, bsz, seq_len, tm):
    rows, d = x2d.shape
    tiles_per_seq = seq_len // tm
    tokc = lambda cblk: (lambda i: (i, cblk))
    scan_spec = lambda w: pl.BlockSpec((2, 1, tm, w), lambda i: (0, i // tiles_per_seq, i % tiles_per_seq, 0))
    const = lambda shape: pl.BlockSpec(shape, lambda i: (0,) * len(shape), pipeline_mode=pl.Buffered(1))
    return pl.pallas_call(
        _merge_kernel,
        grid=(rows // tm,),
        in_specs=[pl.BlockSpec((tm, d), lambda i: (i, 0)),
                  pl.BlockSpec((1, 1, d), lambda i: (i // tiles_per_seq, 0, 0)),
                  scan_spec(A_WIDTH), scan_spec(B_WIDTH),
                  pl.BlockSpec((tm, A_WIDTH), tokc(0)),
                  pl.BlockSpec((tm, A_WIDTH), tokc(1)),
                  pl.BlockSpec((tm, A_WIDTH), tokc(2)),
                  pl.BlockSpec((tm, 2 * LANES), tokc(C_LWA // (2 * LANES))),
                  pl.BlockSpec((tm, A_WIDTH), tokc(C_ZA // A_WIDTH)),
                  pl.BlockSpec((tm, B_WIDTH), tokc(C_ZB // B_WIDTH)),
                  pl.BlockSpec((tm, d), tokc(C_GA // d)),
                  pl.BlockSpec((tm, d), tokc(C_GB // d)),
                  const((2, LANES, A_WIDTH)), const((2, 1, A_WIDTH)), const((1, A_WIDTH)), const((1, A_WIDTH)),
                  const((1, A_WIDTH)), const((1, A_WIDTH)), const((1, B_WIDTH)),
                  const((A_WIDTH, LANES)), const((LANES, A_WIDTH)), const((B_WIDTH, LANES)),
                  const((LANES, B_WIDTH)),
                  const((A_WIDTH, d)), const((B_WIDTH, d)), const((d, d)), const((1, d))],
        out_specs=pl.BlockSpec((tm, d), lambda i: (i, 0)),
        out_shape=jax.ShapeDtypeStruct((rows, d), F32),
        compiler_params=_cparams(("parallel",)),
    )(x2d, gate, y_a, o_b, rkv_c, rkv_c, rkv_c, p, p, p, p, p,
      wts["aa2"], wts["aa0"], wts["a_k_a"], wts["a_r_k"], wts["lnx_g"], wts["lnx_b"], wts["bng"],
      wts["hsum_a"], wts["hexp_a"], wts["hsum_b"], wts["hexp_b"], wts["w_a"], wts["w_b"], wts["w_out"],
      wts["final_g"])


def _head_indicator(width, head):
    m = (jnp.arange(width)[:, None] // head == jnp.arange(LANES)[None, :]).astype(BF16)
    return m, m.T


def _prep_weights(w_in, conv_w, a_w0, a_w2, a_a0, a_a2, a_k_k, a_k_a, a_r_k, a_lnx_g, a_lnx_b, b_gk_w2,
                  b_gk_b, b_norm_g, w_a, w_b, w_out, final_g):
    sizes = (3 * A_WIDTH, A_WIDTH, 2 * A_RANK, 2 * A_RANK, B_KEY, B_KEY, B_WIDTH, B_WIDTH, 2 * B_GATE_RANK,
             D_MODEL, D_MODEL)
    offs = [0]
    for s in sizes:
        offs.append(offs[-1] + s)
    seg = lambda i: w_in[:, offs[i]:offs[i + 1]]
    rkv, z_a, lora_w, lora_a, q_b, k_b, v_b, z_b, lora_g, g_a, g_b = (seg(i) for i in range(11))
    pad_g = jnp.zeros((D_MODEL, LANES - 2 * B_GATE_RANK), w_in.dtype)
    pad_end = jnp.zeros((D_MODEL, P_PAD - C_LG - LANES), w_in.dtype)
    w_perm = jnp.concatenate([rkv, z_a, g_a, g_b, q_b, k_b, v_b, z_b, lora_w, lora_a, lora_g, pad_g, pad_end],
                             axis=1).astype(BF16)

    def dir_pad(w2, rank):
        out = jnp.zeros((2, LANES, w2.shape[-1]), F32)
        for d in range(2):
            out = out.at[d, d * rank:(d + 1) * rank].set(w2[d])
        return out.astype(BF16)

    hsum_a, hexp_a = _head_indicator(A_WIDTH, A_HEAD)
    hsum_b, hexp_b = _head_indicator(B_WIDTH, B_DV)
    return {
        "w_perm": w_perm,
        "conv9": conv_w.reshape(9, 3 * A_WIDTH),
        "aw2": dir_pad(a_w2, A_RANK), "aa2": dir_pad(a_a2, A_RANK),
        "aw0": a_w0.reshape(2, 1, A_WIDTH), "aa0": a_a0.reshape(2, 1, A_WIDTH),
        "a_k_k": a_k_k.reshape(1, A_WIDTH), "a_k_a": a_k_a.reshape(1, A_WIDTH),
        "a_r_k": a_r_k.reshape(1, A_WIDTH),
        "lnx_g": a_lnx_g.reshape(1, A_WIDTH), "lnx_b": a_lnx_b.reshape(1, A_WIDTH),
        "gw2": dir_pad(b_gk_w2, B_GATE_RANK), "gb": b_gk_b.reshape(2, 1, B_KEY),
        "bng": jnp.tile(b_norm_g, B_HEADS).reshape(1, B_WIDTH),
        "hsum_a": hsum_a, "hexp_a": hexp_a, "hsum_b": hsum_b, "hexp_b": hexp_b,
        "w_a": w_a.astype(BF16), "w_b": w_b.astype(BF16), "w_out": w_out.astype(BF16),
        "final_g": final_g.reshape(1, D_MODEL),
    }


def _pick_tile(n, cap):
    t = cap
    while n % t:
        t //= 2
    return t


def kernel(x, c, ctx, c_ctx, w_mod, b_mod, norm_g, w_in, conv_w, a_w0, a_w2, a_a0, a_a2, a_k_k, a_k_a,
           a_r_k, a_lnx_g, a_lnx_b, b_gk_w2, b_gk_b, b_norm_g, w_a, w_b, w_out, final_g):
    assert w_in.shape[0] == 1, "single layer"
    bsz, seq_len, d = x.shape
    ctx_len = ctx.shape[1]
    wts = _prep_weights(w_in[0], conv_w[0], a_w0[0], a_w2[0], a_a0[0], a_a2[0], a_k_k[0], a_k_a[0], a_r_k[0],
                        a_lnx_g[0], a_lnx_b[0], b_gk_w2[0], b_gk_b[0], b_norm_g[0], w_a[0], w_b[0], w_out[0],
                        final_g)

    cond = jnp.concatenate([c, c_ctx[None], jnp.zeros((8 - bsz - 1, d), F32)], axis=0)
    mod = _modulation(cond, w_mod[0], b_mod[0])
    shift, scale, gate = mod[:, :d], mod[:, d:2 * d], mod[:, 2 * d:]
    as_rows = lambda m, lo, hi: m[lo:hi].reshape(hi - lo, 1, d)

    ctx2d = ctx.reshape(bsz * ctx_len, d)
    p_ctx = _inproj(ctx2d, as_rows(scale, bsz, bsz + 1), as_rows(shift, bsz, bsz + 1), norm_g[0],
                    wts["w_perm"], bsz * ctx_len, _pick_tile(bsz * ctx_len, 1024))
    conv_ctx = wts["conv9"] * jnp.array([0.0, 1.0, 0.0], F32).repeat(3)[:, None]
    rkv_ctx = _conv(p_ctx, conv_ctx, ctx_len, False, _pick_tile(ctx_len, 512))
    zero_a = jnp.zeros((2, bsz, A_PAIRS, LANES, LANES), F32)
    zero_b = jnp.zeros((2, bsz, B_HEADS, B_DV, B_DK), F32)
    _, sa_ctx = _rwkv_scan(rkv_ctx, p_ctx, wts, zero_a, bsz, ctx_len)
    _, sb_ctx = _gla_scan(p_ctx, wts, zero_b, bsz, ctx_len)

    x2d = x.reshape(bsz * seq_len, d)
    p = _inproj(x2d, as_rows(scale, 0, bsz), as_rows(shift, 0, bsz), norm_g[0], wts["w_perm"], seq_len,
                _pick_tile(seq_len, 1024))
    rkv_c = _conv(p, wts["conv9"], seq_len, True, _pick_tile(seq_len, 512))
    y_a, _ = _rwkv_scan(rkv_c, p, wts, sa_ctx, bsz, seq_len)
    o_b, _ = _gla_scan(p, wts, sb_ctx, bsz, seq_len)
    out = _merge(x2d, as_rows(gate, 0, bsz), y_a, o_b, rkv_c, p, wts, bsz, seq_len, _pick_tile(seq_len, 256))
    return out.reshape(bsz, seq_len, d)
```

```python
import functools
import math

import jax
import jax.numpy as jnp
from jax import lax
from jax.experimental import pallas as pl
from jax.experimental.pallas import tpu as pltpu

F32 = jnp.float32
BF16 = jnp.bfloat16

D_MODEL = 2048
GRID_W = 64
EPS = 1e-6

A_WIDTH = 1024
A_HEAD = 64
A_HEADS = 16
A_PAIRS = A_HEADS // 2
A_RANK = 64
A_LN_EPS = 64e-5

B_WIDTH = 1024
B_HEADS = 4
B_KEY = 512
B_DK = 128
B_DV = 256
B_GATE_RANK = 16
B_GATE_NORM = 16.0

CHUNK = 64
LANES = 128

C_RKV = 0
C_ZA = 3072
C_GA = 4096
C_GB = 6144
C_QK = 8192
C_VB = 9216
C_ZB = 10240
C_LWA = 11264
C_LG = 11520
P_PAD = 11776

VMEM_LIMIT = 56 * 1024 * 1024


def _cparams(sem):
    return pltpu.CompilerParams(dimension_semantics=sem, vmem_limit_bytes=VMEM_LIMIT)


def _dot(a, b):
    return jnp.dot(a.astype(BF16), b.astype(BF16), preferred_element_type=F32)


def _dot_nt(a, b):
    return lax.dot_general(a.astype(BF16), b.astype(BF16), (((1,), (1,)), ((), ())),
                           preferred_element_type=F32)


def _split(a):
    hi = a.astype(BF16)
    lo = (a - hi.astype(F32)).astype(BF16)
    return hi, lo


def _dot_exact_lhs(a_bf16, b):
    bh, bl = _split(b)
    d = functools.partial(jnp.dot, preferred_element_type=F32)
    return d(a_bf16, bh) + d(a_bf16, bl)


def _sigmoid(x):
    return 1.0 / (1.0 + jnp.exp(-x))


def _silu(x):
    return x * _sigmoid(x)


def _mod_kernel(c_ref, w_ref, b_ref, o_ref):
    o_ref[...] = _dot(_silu(c_ref[...]), w_ref[...]) + b_ref[...]


def _modulation(cond, w_mod, b_mod):
    rows, d = cond.shape
    n = w_mod.shape[1]
    tn = 1024
    return pl.pallas_call(
        _mod_kernel,
        grid=(n // tn,),
        in_specs=[pl.BlockSpec((rows, d), lambda j: (0, 0)),
                  pl.BlockSpec((d, tn), lambda j: (0, j)),
                  pl.BlockSpec((1, tn), lambda j: (0, j))],
        out_specs=pl.BlockSpec((rows, tn), lambda j: (0, j)),
        out_shape=jax.ShapeDtypeStruct((rows, n), F32),
        compiler_params=_cparams(("parallel",)),
    )(cond, w_mod, b_mod.reshape(1, n))


def _inproj_kernel(x_ref, sc_ref, sh_ref, g_ref, w_ref, o_ref, h_ref):
    @pl.when(pl.program_id(1) == 0)
    def _():
        x = x_ref[...]
        y = x * lax.rsqrt(jnp.mean(x * x, axis=-1, keepdims=True) + EPS)
        h = (y * g_ref[...]) * (1.0 + sc_ref[0]) + sh_ref[0]
        h_ref[...] = h.astype(BF16)

    o_ref[...] = jnp.dot(h_ref[...], w_ref[...], preferred_element_type=F32).astype(o_ref.dtype)


def _inproj(x2d, scale, shift, norm_g, w_perm, rows_per_mod, tm):
    rows, d = x2d.shape
    tn = 512
    tiles_per_mod = rows_per_mod // tm
    return pl.pallas_call(
        _inproj_kernel,
        grid=(rows // tm, P_PAD // tn),
        in_specs=[pl.BlockSpec((tm, d), lambda i, j: (i, 0)),
                  pl.BlockSpec((1, 1, d), lambda i, j: (i // tiles_per_mod, 0, 0)),
                  pl.BlockSpec((1, 1, d), lambda i, j: (i // tiles_per_mod, 0, 0)),
                  pl.BlockSpec((1, d), lambda i, j: (0, 0)),
                  pl.BlockSpec((d, tn), lambda i, j: (0, j))],
        out_specs=pl.BlockSpec((tm, tn), lambda i, j: (i, j)),
        out_shape=jax.ShapeDtypeStruct((rows, P_PAD), BF16),
        scratch_shapes=[pltpu.VMEM((tm, d), BF16)],
        compiler_params=_cparams(("parallel", "arbitrary")),
    )(x2d, scale, shift, norm_g.reshape(1, d), w_perm)


CONV_HALO = 128


def _conv_kernel(main_ref, prev_ref, next_ref, w_ref, o_ref, buf_ref, *, tiles_per_seq, on_grid):
    tt = main_ref.shape[0]
    pos = pl.program_id(0) % tiles_per_seq
    prev = prev_ref[...].astype(F32)
    nxt = next_ref[...].astype(F32)
    buf_ref[0:CONV_HALO, :] = jnp.where(pos == 0, 0.0, prev)
    buf_ref[CONV_HALO:CONV_HALO + tt, :] = main_ref[...].astype(F32)
    buf_ref[CONV_HALO + tt:, :] = jnp.where(pos == tiles_per_seq - 1, 0.0, nxt)
    col = lax.broadcasted_iota(jnp.int32, o_ref.shape, 0) % GRID_W
    acc = None
    for dc in range(3):
        part = None
        for dr in range(3):
            off = (dr - 1) * GRID_W + (dc - 1)
            term = buf_ref[CONV_HALO + off:CONV_HALO + off + tt, :] * w_ref[dr * 3 + dc:dr * 3 + dc + 1, :]
            part = term if part is None else part + term
        if on_grid and dc == 0:
            part = jnp.where(col == 0, 0.0, part)
        if on_grid and dc == 2:
            part = jnp.where(col == GRID_W - 1, 0.0, part)
        acc = part if acc is None else acc + part
    o_ref[...] = acc.astype(o_ref.dtype)


def _conv(p, w9, seq_len, on_grid, tt):
    rows = p.shape[0]
    ct = 512
    width = 3 * A_WIDTH
    tiles_per_seq = seq_len // tt
    hb = tt // CONV_HALO
    last_hb = rows // CONV_HALO - 1
    kern = functools.partial(_conv_kernel, tiles_per_seq=tiles_per_seq, on_grid=on_grid)
    return pl.pallas_call(
        kern,
        grid=(rows // tt, width // ct),
        in_specs=[pl.BlockSpec((tt, ct), lambda i, j: (i, j)),
                  pl.BlockSpec((CONV_HALO, ct), lambda i, j: (jnp.maximum(i * hb - 1, 0), j)),
                  pl.BlockSpec((CONV_HALO, ct), lambda i, j: (jnp.minimum((i + 1) * hb, last_hb), j)),
                  pl.BlockSpec((9, ct), lambda i, j: (0, j))],
        out_specs=pl.BlockSpec((tt, ct), lambda i, j: (i, j)),
        out_shape=jax.ShapeDtypeStruct((rows, width), BF16),
        scratch_shapes=[pltpu.VMEM((tt + 2 * CONV_HALO, ct), F32)],
        compiler_params=_cparams(("parallel", "parallel")),
    )(p, p, p, w9)


def _chunk_index(d, n, n_chunks):
    return n + d * (n_chunks - 1 - 2 * n)


def _rwkv_features(d, out, rkv_ref, lora_ref, aw2_ref, aa2_ref, aw0_ref, aa0_ref, kk_ref, ka_ref, hsum_ref,
                   hexp_ref):
    r = rkv_ref[:, :A_WIDTH].astype(F32)
    k = rkv_ref[:, A_WIDTH:2 * A_WIDTH].astype(F32)
    lora = lora_ref[...]
    lw = jnp.tanh(lora[:, :LANES].astype(F32))
    w_log = aw0_ref[d] + _dot(lw, aw2_ref[d])
    iclr_pre = aa0_ref[d] + _dot(lora[:, LANES:], aa2_ref[d])
    kk = k * kk_ref[...]
    ss = _dot(kk * kk, hsum_ref[...])
    yield
    logw = -math.exp(-0.5) * _sigmoid(w_log)
    iclr = _sigmoid(iclr_pre)
    ss = _dot(ss, hexp_ref[...])
    row = lax.broadcasted_iota(jnp.int32, (CHUNK, CHUNK), 0)
    col = lax.broadcasted_iota(jnp.int32, (CHUNK, CHUNK), 1)
    tri = jnp.where((col <= row) if d == 0 else (col >= row), 1.0, 0.0).astype(BF16)
    cum = _dot_exact_lhs(tri, logw)
    tot = jnp.sum(logw, axis=0, keepdims=True)
    yield
    kk = kk * lax.rsqrt(ss + 1e-12)
    kdir = k * (1.0 + (iclr - 1.0) * ka_ref[...])
    b = kk * iclr
    yield
    e_neg = jnp.exp(-cum)
    out["a_t"] = -kk * jnp.exp(cum - logw)
    out["r_t"] = r * jnp.exp(cum)
    yield
    out["b_t"] = b * e_neg
    out["k_t"] = kdir * e_neg
    yield
    e_rel = jnp.exp(tot - cum)
    out["b_h"] = b * e_rel
    out["k_h"] = kdir * e_rel
    out["p_tot"] = jnp.exp(tot)
    out["v"] = rkv_ref[:, 2 * A_WIDTH:].astype(F32)


def _rwkv_pairs(d, f, y_ref, st_ref, filler=()):
    filler = iter(filler)
    tick = lambda: next(filler, None)
    prow = lax.broadcasted_iota(jnp.int32, (CHUNK, LANES), 0)
    pcol = lax.broadcasted_iota(jnp.int32, (CHUNK, LANES), 1) % CHUNK
    before = (pcol < prow) if d == 0 else (pcol > prow)
    upto = (pcol <= prow) if d == 0 else (pcol >= prow)
    brow = lax.broadcasted_iota(jnp.int32, (LANES, LANES), 0)
    bcol = lax.broadcasted_iota(jnp.int32, (LANES, LANES), 1)
    same_head = (brow // CHUNK) == (bcol // CHUNK)
    eye = brow == bcol

    def bd(x):
        return jnp.where(same_head, jnp.concatenate([x, x], axis=0), 0.0)

    a_t, r_t, b_t, k_t, b_h, k_h, v, p_tot = (f[name] for name in
                                               ("a_t", "r_t", "b_t", "k_t", "b_h", "k_h", "v", "p_tot"))
    sls = [slice(p * LANES, (p + 1) * LANES) for p in range(A_PAIRS)]
    each = lambda fn, *lists: [fn(*xs) for xs in zip(*lists)]
    mm = [_dot_nt(jnp.concatenate([a_t[:, sl], r_t[:, sl]], axis=0),
                  jnp.concatenate([bd(b_t[:, sl]), bd(k_t[:, sl])], axis=0)) for sl in sls]
    n_bd = [bd(jnp.where(before, m[:CHUNK, :LANES], 0.0)) for m in mm]
    mak_bd = [bd(jnp.where(before, m[:CHUNK, LANES:], 0.0)) for m in mm]
    qb_bd = [bd(jnp.where(upto, m[CHUNK:, :LANES], 0.0)) for m in mm]
    qk_bd = [bd(jnp.where(upto, m[CHUNK:, LANES:], 0.0)) for m in mm]
    v_bd = [bd(v[:, sl]) for sl in sls]
    tick()
    mq = each(lambda mk, qk, vb: _dot(jnp.concatenate([mk, qk], axis=0), vb), mak_bd, qk_bd, v_bd)
    tick()

    t_inv = [jnp.where(eye, 1.0, 0.0) + nb for nb in n_bd]
    x = [_dot(nb, nb) for nb in n_bd]
    for _ in range(4):
        z = each(lambda xx, tt: _dot(xx, jnp.concatenate([xx, tt], axis=1)), x, t_inv)
        x = [zz[:, :LANES] for zz in z]
        t_inv = each(lambda tt, zz: tt + zz[:, LANES:], t_inv, z)
        tick()
    t_inv = each(lambda tt, xx: tt + _dot(xx, tt), t_inv, x)
    for _ in filler:
        pass

    wu = each(lambda tt, sl, m: _dot(tt, jnp.concatenate([bd(a_t[:, sl]), m[:LANES]], axis=1)),
              t_inv, sls, mq)
    ry = each(lambda qb, w, sl, m: _dot(qb, w) + jnp.concatenate([bd(r_t[:, sl]), m[LANES:]], axis=1),
              qb_bd, wu, sls, mq)
    zeros = jnp.zeros((LANES, LANES), F32)
    gh = each(lambda sl, w, vb: _dot(jnp.concatenate([bd(b_h[:, sl]), bd(k_h[:, sl])], axis=0).T,
                                     jnp.concatenate([w, jnp.concatenate([zeros, vb], axis=1)], axis=0)),
              sls, wu, v_bd)
    for p, (sl, ry_p, gh_p) in enumerate(zip(sls, ry, gh)):
        g = gh_p[:, :LANES] + jnp.where(eye, p_tot[:, sl], 0.0)
        rg = _dot(jnp.concatenate([ry_p[:, :LANES], g], axis=0), st_ref[d, p])
        y = rg[:LANES] + ry_p[:, LANES:]
        y_ref[0, :, sl] = (y[:CHUNK] + y[CHUNK:]).astype(y_ref.dtype)
        st_ref[d, p] = rg[LANES:] + gh_p[:, LANES:]


def _rwkv_kernel(rkvf_ref, loraf_ref, rkvr_ref, lorar_ref, aw2_ref, aa2_ref, aw0_ref, aa0_ref, kk_ref, ka_ref,
                 hsum_ref, hexp_ref, s0_ref, yf_ref, yr_ref, sfin_ref, st_ref):
    n = pl.program_id(1)

    @pl.when(n == 0)
    def _():
        st_ref[...] = s0_ref[:, 0]

    wrefs = (aw2_ref, aa2_ref, aw0_ref, aa0_ref, kk_ref, ka_ref, hsum_ref, hexp_ref)
    feat_f, feat_r = {}, {}
    for _ in _rwkv_features(0, feat_f, rkvf_ref, loraf_ref, *wrefs):
        pass
    _rwkv_pairs(0, feat_f, yf_ref, st_ref, filler=_rwkv_features(1, feat_r, rkvr_ref, lorar_ref, *wrefs))
    _rwkv_pairs(1, feat_r, yr_ref, st_ref)

    @pl.when(n == pl.num_programs(1) - 1)
    def _():
        sfin_ref[:, 0] = st_ref[...]


def _rwkv_scan(rkv_c, p, wts, s0, bsz, seq_len):
    n_chunks = seq_len // CHUNK

    def tok_specs(d):
        row = lambda b, n: b * n_chunks + _chunk_index(d, n, n_chunks)
        return [pl.BlockSpec((CHUNK, 3 * A_WIDTH), lambda b, n: (row(b, n), 0)),
                pl.BlockSpec((CHUNK, 2 * LANES), lambda b, n: (row(b, n), C_LWA // (2 * LANES)))]

    const = lambda shape: pl.BlockSpec(shape, lambda b, n: (0,) * len(shape))
    state_spec = pl.BlockSpec((2, 1, A_PAIRS, LANES, LANES), lambda b, n: (0, b, 0, 0, 0))
    state_shape = jax.ShapeDtypeStruct((2, bsz, A_PAIRS, LANES, LANES), F32)
    out_spec = lambda d: pl.BlockSpec((1, CHUNK, A_WIDTH), lambda b, n: (b, _chunk_index(d, n, n_chunks), 0))
    out_shape = jax.ShapeDtypeStruct((bsz, seq_len, A_WIDTH), F32)
    return pl.pallas_call(
        _rwkv_kernel,
        grid=(bsz, n_chunks),
        in_specs=tok_specs(0) + tok_specs(1) + [
            const((2, LANES, A_WIDTH)), const((2, LANES, A_WIDTH)), const((2, 1, A_WIDTH)),
            const((2, 1, A_WIDTH)), const((1, A_WIDTH)), const((1, A_WIDTH)), const((A_WIDTH, LANES)),
            const((LANES, A_WIDTH)), state_spec],
        out_specs=[out_spec(0), out_spec(1), state_spec],
        out_shape=[out_shape, out_shape, state_shape],
        scratch_shapes=[pltpu.VMEM((2, A_PAIRS, LANES, LANES), F32)],
        compiler_params=_cparams(("parallel", "arbitrary")),
    )(rkv_c, p, rkv_c, p, wts["aw2"], wts["aa2"], wts["aw0"], wts["aa0"], wts["a_k_k"], wts["a_k_a"],
      wts["hsum_a"], wts["hexp_a"], s0)


def _gla_kernel(qkvf_ref, lgf_ref, qkvr_ref, lgr_ref, w2_ref, bg_ref, s0_ref, of_ref, or_ref, sfin_ref,
                st_ref):
    n = pl.program_id(1)

    @pl.when(n == 0)
    def _():
        st_ref[...] = s0_ref[:, 0]

    dirs = (0, 1)
    qkv_refs = (qkvf_ref, qkvr_ref)
    o_refs = (of_ref, or_ref)
    row = lax.broadcasted_iota(jnp.int32, (CHUNK, CHUNK), 0)
    col = lax.broadcasted_iota(jnp.int32, (CHUNK, CHUNK), 1)
    upto = [col <= row, col >= row]
    z = [_dot(lg_ref[...], w2_ref[d]) + bg_ref[d] for d, lg_ref in zip(dirs, (lgf_ref, lgr_ref))]
    gk = [(jnp.minimum(zz, 0.0) - jnp.log1p(jnp.exp(-jnp.abs(zz)))) * (1.0 / B_GATE_NORM) for zz in z]
    cum = [_dot_exact_lhs(jnp.where(upto[d], 1.0, 0.0).astype(BF16), gk[d]) for d in dirs]
    tot = [jnp.sum(g, axis=0, keepdims=True) for g in gk]
    q = [ref[:, :B_KEY].astype(F32) for ref in qkv_refs]
    k = [ref[:, B_KEY:2 * B_KEY].astype(F32) for ref in qkv_refs]
    q_dec = [(q[d] * (B_DK ** -0.5) * jnp.exp(cum[d])).astype(BF16) for d in dirs]
    k_inv = [(k[d] * jnp.exp(-cum[d])).astype(BF16) for d in dirs]
    k_state = [(k[d] * jnp.exp(tot[d] - cum[d])).astype(BF16) for d in dirs]
    decay = [jnp.exp(t) for t in tot]

    chains = [(d, h) for d in dirs for h in range(B_HEADS)]
    ks = lambda h: slice(h * B_DK, (h + 1) * B_DK)
    vs = lambda h: slice(h * B_DV, (h + 1) * B_DV)
    v = {(d, h): qkv_refs[d][:, 2 * B_KEY + h * B_DV:2 * B_KEY + (h + 1) * B_DV] for d, h in chains}
    att = [jnp.where(upto[d], _dot_nt(q_dec[d][:, ks(h)], k_inv[d][:, ks(h)]), 0.0) for d, h in chains]
    s_t = [st_ref[d, h] for d, h in chains]
    outs = [_dot_nt(q_dec[d][:, ks(h)], s) + _dot(a, v[d, h]) for (d, h), s, a in zip(chains, s_t, att)]
    for (d, h), o in zip(chains, outs):
        o_refs[d][0, :, vs(h)] = o.astype(o_refs[d].dtype)
    for (d, h), s in zip(chains, s_t):
        st_ref[d, h] = s * decay[d][:, ks(h)] + _dot(v[d, h].astype(F32).T, k_state[d][:, ks(h)])

    @pl.when(n == pl.num_programs(1) - 1)
    def _():
        sfin_ref[:, 0] = st_ref[...]


def _gla_scan(p, wts, s0, bsz, seq_len):
    n_chunks = seq_len // CHUNK
    qkv_w = 2 * B_KEY + B_WIDTH

    def tok_specs(d):
        row = lambda b, n: b * n_chunks + _chunk_index(d, n, n_chunks)
        return [pl.BlockSpec((CHUNK, qkv_w), lambda b, n: (row(b, n), C_QK // qkv_w)),
                pl.BlockSpec((CHUNK, LANES), lambda b, n: (row(b, n), C_LG // LANES))]

    const3 = lambda b, n: (0, 0, 0)
    state_spec = pl.BlockSpec((2, 1, B_HEADS, B_DV, B_DK), lambda b, n: (0, b, 0, 0, 0))
    state_shape = jax.ShapeDtypeStruct((2, bsz, B_HEADS, B_DV, B_DK), F32)
    out_spec = lambda d: pl.BlockSpec((1, CHUNK, B_WIDTH), lambda b, n: (b, _chunk_index(d, n, n_chunks), 0))
    out_shape = jax.ShapeDtypeStruct((bsz, seq_len, B_WIDTH), F32)
    return pl.pallas_call(
        _gla_kernel,
        grid=(bsz, n_chunks),
        in_specs=tok_specs(0) + tok_specs(1) + [pl.BlockSpec((2, LANES, B_KEY), const3),
                                                pl.BlockSpec((2, 1, B_KEY), const3),
                                                state_spec],
        out_specs=[out_spec(0), out_spec(1), state_spec],
        out_shape=[out_shape, out_shape, state_shape],
        scratch_shapes=[pltpu.VMEM((2, B_HEADS, B_DV, B_DK), F32)],
        compiler_params=_cparams(("parallel", "arbitrary")),
    )(p, p, p, p, wts["gw2"], wts["gb"], s0)


def _merge_kernel(x_ref, gate_ref, yaf_ref, yar_ref, obf_ref, obr_ref, r_ref, k_ref, v_ref, la_ref, za_ref,
                  zb_ref, ga_ref, gb_ref, aa2_ref, aa0_ref, ka_ref, rk_ref, lng_ref, lnb_ref, bng_ref, hsa_ref,
                  hea_ref, hsb_ref, heb_ref, wa_ref, wb_ref, wo_ref, fg_ref, o_ref):
    wkv = yaf_ref[0] + yar_ref[0]
    mu = _dot(_dot(wkv, hsa_ref[...]), hea_ref[...]) * (1.0 / A_HEAD)
    cen = wkv - mu
    var = _dot(_dot(cen * cen, hsa_ref[...]), hea_ref[...]) * (1.0 / A_HEAD)
    y_a = cen * lax.rsqrt(var + A_LN_EPS) * lng_ref[...] + lnb_ref[...]
    r = r_ref[...].astype(F32)
    k = k_ref[...].astype(F32)
    v = v_ref[...].astype(F32)
    la = la_ref[...][:, LANES:]
    iclr_sum = (_sigmoid(aa0_ref[0] + _dot(la, aa2_ref[0])) + _sigmoid(aa0_ref[1] + _dot(la, aa2_ref[1])))
    rk = r * k * rk_ref[...] * (2.0 + (iclr_sum - 2.0) * ka_ref[...])
    bonus = _dot(_dot(rk, hsa_ref[...]), hea_ref[...])
    y_a = (y_a + bonus * v) * _silu(za_ref[...].astype(F32))
    ob = obf_ref[0] + obr_ref[0]
    ms = _dot(_dot(ob * ob, hsb_ref[...]), heb_ref[...]) * (1.0 / B_DV)
    y_b = ob * lax.rsqrt(ms + EPS) * bng_ref[...] * _silu(zb_ref[...].astype(F32))
    mixed = (_sigmoid(ga_ref[...].astype(F32)) * _dot(y_a, wa_ref[...])
             + _sigmoid(gb_ref[...].astype(F32)) * _dot(y_b, wb_ref[...]))
    y = _dot(mixed, wo_ref[...])
    xo = x_ref[...] + gate_ref[0] * y
    o_ref[...] = xo * lax.rsqrt(jnp.mean(xo * xo, axis=-1, keepdims=True) + EPS) * fg_ref[...]


def _merge(x2d, gate, y_af, y_ar, o_bf, o_br, rkv_c, p, wts, bsz, seq_len, tm):
    rows, d = x2d.shape
    tiles_per_seq = seq_len // tm
    tokc = lambda cblk: (lambda i: (i, cblk))
    scan_spec = pl.BlockSpec((1, tm, A_WIDTH), lambda i: (i // tiles_per_seq, i % tiles_per_seq, 0))
    const = lambda shape: pl.BlockSpec(shape, lambda i: (0,) * len(shape), pipeline_mode=pl.Buffered(1))
    return pl.pallas_call(
        _merge_kernel,
        grid=(rows // tm,),
        in_specs=[pl.BlockSpec((tm, d), lambda i: (i, 0)),
                  pl.BlockSpec((1, 1, d), lambda i: (i // tiles_per_seq, 0, 0)),
                  scan_spec, scan_spec, scan_spec, scan_spec,
                  pl.BlockSpec((tm, A_WIDTH), tokc(0)),
                  pl.BlockSpec((tm, A_WIDTH), tokc(1)),
                  pl.BlockSpec((tm, A_WIDTH), tokc(2)),
                  pl.BlockSpec((tm, 2 * LANES), tokc(C_LWA // (2 * LANES))),
                  pl.BlockSpec((tm, A_WIDTH), tokc(C_ZA // A_WIDTH)),
                  pl.BlockSpec((tm, B_WIDTH), tokc(C_ZB // B_WIDTH)),
                  pl.BlockSpec((tm, d), tokc(C_GA // d)),
                  pl.BlockSpec((tm, d), tokc(C_GB // d)),
                  const((2, LANES, A_WIDTH)), const((2, 1, A_WIDTH)), const((1, A_WIDTH)), const((1, A_WIDTH)),
                  const((1, A_WIDTH)), const((1, A_WIDTH)), const((1, B_WIDTH)),
                  const((A_WIDTH, LANES)), const((LANES, A_WIDTH)), const((B_WIDTH, LANES)),
                  const((LANES, B_WIDTH)),
                  const((A_WIDTH, d)), const((B_WIDTH, d)), const((d, d)), const((1, d))],
        out_specs=pl.BlockSpec((tm, d), lambda i: (i, 0)),
        out_shape=jax.ShapeDtypeStruct((rows, d), F32),
        compiler_params=_cparams(("parallel",)),
    )(x2d, gate, y_af, y_ar, o_bf, o_br, rkv_c, rkv_c, rkv_c, p, p, p, p, p,
      wts["aa2"], wts["aa0"], wts["a_k_a"], wts["a_r_k"], wts["lnx_g"], wts["lnx_b"], wts["bng"],
      wts["hsum_a"], wts["hexp_a"], wts["hsum_b"], wts["hexp_b"], wts["w_a"], wts["w_b"], wts["w_out"],
      wts["final_g"])


def _head_indicator(width, head):
    m = (jnp.arange(width)[:, None] // head == jnp.arange(LANES)[None, :]).astype(BF16)
    return m, m.T


def _prep_weights(w_in, conv_w, a_w0, a_w2, a_a0, a_a2, a_k_k, a_k_a, a_r_k, a_lnx_g, a_lnx_b, b_gk_w2,
                  b_gk_b, b_norm_g, w_a, w_b, w_out, final_g):
    sizes = (3 * A_WIDTH, A_WIDTH, 2 * A_RANK, 2 * A_RANK, B_KEY, B_KEY, B_WIDTH, B_WIDTH, 2 * B_GATE_RANK,
             D_MODEL, D_MODEL)
    offs = [0]
    for s in sizes:
        offs.append(offs[-1] + s)
    seg = lambda i: w_in[:, offs[i]:offs[i + 1]]
    rkv, z_a, lora_w, lora_a, q_b, k_b, v_b, z_b, lora_g, g_a, g_b = (seg(i) for i in range(11))
    pad_g = jnp.zeros((D_MODEL, LANES - 2 * B_GATE_RANK), w_in.dtype)
    pad_end = jnp.zeros((D_MODEL, P_PAD - C_LG - LANES), w_in.dtype)
    w_perm = jnp.concatenate([rkv, z_a, g_a, g_b, q_b, k_b, v_b, z_b, lora_w, lora_a, lora_g, pad_g, pad_end],
                             axis=1).astype(BF16)

    def dir_pad(w2, rank):
        out = jnp.zeros((2, LANES, w2.shape[-1]), F32)
        for d in range(2):
            out = out.at[d, d * rank:(d + 1) * rank].set(w2[d])
        return out.astype(BF16)

    hsum_a, hexp_a = _head_indicator(A_WIDTH, A_HEAD)
    hsum_b, hexp_b = _head_indicator(B_WIDTH, B_DV)
    return {
        "w_perm": w_perm,
        "conv9": conv_w.reshape(9, 3 * A_WIDTH),
        "aw2": dir_pad(a_w2, A_RANK), "aa2": dir_pad(a_a2, A_RANK),
        "aw0": a_w0.reshape(2, 1, A_WIDTH), "aa0": a_a0.reshape(2, 1, A_WIDTH),
        "a_k_k": a_k_k.reshape(1, A_WIDTH), "a_k_a": a_k_a.reshape(1, A_WIDTH),
        "a_r_k": a_r_k.reshape(1, A_WIDTH),
        "lnx_g": a_lnx_g.reshape(1, A_WIDTH), "lnx_b": a_lnx_b.reshape(1, A_WIDTH),
        "gw2": dir_pad(b_gk_w2, B_GATE_RANK), "gb": b_gk_b.reshape(2, 1, B_KEY),
        "bng": jnp.tile(b_norm_g, B_HEADS).reshape(1, B_WIDTH),
        "hsum_a": hsum_a, "hexp_a": hexp_a, "hsum_b": hsum_b, "hexp_b": hexp_b,
        "w_a": w_a.astype(BF16), "w_b": w_b.astype(BF16), "w_out": w_out.astype(BF16),
        "final_g": final_g.reshape(1, D_MODEL),
    }


def _pick_tile(n, cap):
    t = cap
    while n % t:
        t //= 2
    return t


def kernel(x, c, ctx, c_ctx, w_mod, b_mod, norm_g, w_in, conv_w, a_w0, a_w2, a_a0, a_a2, a_k_k, a_k_a,
           a_r_k, a_lnx_g, a_lnx_b, b_gk_w2, b_gk_b, b_norm_g, w_a, w_b, w_out, final_g):
    assert w_in.shape[0] == 1, "single layer"
    bsz, seq_len, d = x.shape
    ctx_len = ctx.shape[1]
    wts = _prep_weights(w_in[0], conv_w[0], a_w0[0], a_w2[0], a_a0[0], a_a2[0], a_k_k[0], a_k_a[0], a_r_k[0],
                        a_lnx_g[0], a_lnx_b[0], b_gk_w2[0], b_gk_b[0], b_norm_g[0], w_a[0], w_b[0], w_out[0],
                        final_g)

    cond = jnp.concatenate([c, c_ctx[None], jnp.zeros((8 - bsz - 1, d), F32)], axis=0)
    mod = _modulation(cond, w_mod[0], b_mod[0])
    shift, scale, gate = mod[:, :d], mod[:, d:2 * d], mod[:, 2 * d:]
    as_rows = lambda m, lo, hi: m[lo:hi].reshape(hi - lo, 1, d)

    ctx2d = ctx.reshape(bsz * ctx_len, d)
    p_ctx = _inproj(ctx2d, as_rows(scale, bsz, bsz + 1), as_rows(shift, bsz, bsz + 1), norm_g[0],
                    wts["w_perm"], bsz * ctx_len, _pick_tile(bsz * ctx_len, 1024))
    conv_ctx = wts["conv9"] * jnp.array([0.0, 1.0, 0.0], F32).repeat(3)[:, None]
    rkv_ctx = _conv(p_ctx, conv_ctx, ctx_len, False, _pick_tile(ctx_len, 512))
    zero_a = jnp.zeros((2, bsz, A_PAIRS, LANES, LANES), F32)
    zero_b = jnp.zeros((2, bsz, B_HEADS, B_DV, B_DK), F32)
    _, _, sa_ctx = _rwkv_scan(rkv_ctx, p_ctx, wts, zero_a, bsz, ctx_len)
    _, _, sb_ctx = _gla_scan(p_ctx, wts, zero_b, bsz, ctx_len)

    x2d = x.reshape(bsz * seq_len, d)
    p = _inproj(x2d, as_rows(scale, 0, bsz), as_rows(shift, 0, bsz), norm_g[0], wts["w_perm"], seq_len,
                _pick_tile(seq_len, 1024))
    rkv_c = _conv(p, wts["conv9"], seq_len, True, _pick_tile(seq_len, 512))
    y_af, y_ar, _ = _rwkv_scan(rkv_c, p, wts, sa_ctx, bsz, seq_len)
    o_bf, o_br, _ = _gla_scan(p, wts, sb_ctx, bsz, seq_len)
    out = _merge(x2d, as_rows(gate, 0, bsz), y_af, y_ar, o_bf, o_br, rkv_c, p, wts, bsz, seq_len,
                 _pick_tile(seq_len, 256))
    return out.reshape(bsz, seq_len, d)
```

```python
import functools
import math

import jax
import jax.numpy as jnp
from jax import lax
from jax.experimental import pallas as pl
from jax.experimental.pallas import tpu as pltpu

F32 = jnp.float32
BF16 = jnp.bfloat16

D_MODEL = 2048
GRID_W = 64
EPS = 1e-6

A_WIDTH = 1024
A_HEAD = 64
A_HEADS = 16
A_PAIRS = A_HEADS // 2
A_RANK = 64
A_LN_EPS = 64e-5

B_WIDTH = 1024
B_HEADS = 4
B_KEY = 512
B_DK = 128
B_DV = 256
B_GATE_RANK = 16
B_GATE_NORM = 16.0

CHUNK = 64
LANES = 128

C_RKV = 0
C_ZA = 3072
C_GA = 4096
C_GB = 6144
C_QK = 8192
C_VB = 9216
C_ZB = 10240
C_LWA = 11264
C_LG = 11520
P_PAD = 11776

VMEM_LIMIT = 56 * 1024 * 1024


def _cparams(sem):
    return pltpu.CompilerParams(dimension_semantics=sem, vmem_limit_bytes=VMEM_LIMIT)


def _dot(a, b):
    return jnp.dot(a.astype(BF16), b.astype(BF16), preferred_element_type=F32)


def _dot_nt(a, b):
    return lax.dot_general(a.astype(BF16), b.astype(BF16), (((1,), (1,)), ((), ())),
                           preferred_element_type=F32)


def _split(a):
    hi = a.astype(BF16)
    lo = (a - hi.astype(F32)).astype(BF16)
    return hi, lo


def _dot_exact_lhs(a_bf16, b):
    bh, bl = _split(b)
    d = functools.partial(jnp.dot, preferred_element_type=F32)
    return d(a_bf16, bh) + d(a_bf16, bl)


def _sigmoid(x):
    return 1.0 / (1.0 + jnp.exp(-x))


def _silu(x):
    return x * _sigmoid(x)


def _mod_kernel(c_ref, w_ref, b_ref, o_ref):
    o_ref[...] = _dot(_silu(c_ref[...]), w_ref[...]) + b_ref[...]


def _modulation(cond, w_mod, b_mod):
    rows, d = cond.shape
    n = w_mod.shape[1]
    tn = 1024
    return pl.pallas_call(
        _mod_kernel,
        grid=(n // tn,),
        in_specs=[pl.BlockSpec((rows, d), lambda j: (0, 0)),
                  pl.BlockSpec((d, tn), lambda j: (0, j)),
                  pl.BlockSpec((1, tn), lambda j: (0, j))],
        out_specs=pl.BlockSpec((rows, tn), lambda j: (0, j)),
        out_shape=jax.ShapeDtypeStruct((rows, n), F32),
        compiler_params=_cparams(("parallel",)),
    )(cond, w_mod, b_mod.reshape(1, n))


def _inproj_kernel(x_ref, sc_ref, sh_ref, g_ref, w_ref, o_ref, h_ref):
    @pl.when(pl.program_id(1) == 0)
    def _():
        x = x_ref[...]
        y = x * lax.rsqrt(jnp.mean(x * x, axis=-1, keepdims=True) + EPS)
        h = (y * g_ref[...]) * (1.0 + sc_ref[0]) + sh_ref[0]
        h_ref[...] = h.astype(BF16)

    o_ref[...] = jnp.dot(h_ref[...], w_ref[...], preferred_element_type=F32).astype(o_ref.dtype)


def _inproj(x2d, scale, shift, norm_g, w_perm, rows_per_mod, tm):
    rows, d = x2d.shape
    tn = 512
    tiles_per_mod = rows_per_mod // tm
    return pl.pallas_call(
        _inproj_kernel,
        grid=(rows // tm, P_PAD // tn),
        in_specs=[pl.BlockSpec((tm, d), lambda i, j: (i, 0)),
                  pl.BlockSpec((1, 1, d), lambda i, j: (i // tiles_per_mod, 0, 0)),
                  pl.BlockSpec((1, 1, d), lambda i, j: (i // tiles_per_mod, 0, 0)),
                  pl.BlockSpec((1, d), lambda i, j: (0, 0)),
                  pl.BlockSpec((d, tn), lambda i, j: (0, j))],
        out_specs=pl.BlockSpec((tm, tn), lambda i, j: (i, j)),
        out_shape=jax.ShapeDtypeStruct((rows, P_PAD), BF16),
        scratch_shapes=[pltpu.VMEM((tm, d), BF16)],
        compiler_params=_cparams(("parallel", "arbitrary")),
    )(x2d, scale, shift, norm_g.reshape(1, d), w_perm)


CONV_HALO = 128


def _conv_kernel(main_ref, prev_ref, next_ref, w_ref, o_ref, buf_ref, *, tiles_per_seq, on_grid):
    tt = main_ref.shape[0]
    pos = pl.program_id(0) % tiles_per_seq
    prev = prev_ref[...].astype(F32)
    nxt = next_ref[...].astype(F32)
    buf_ref[0:CONV_HALO, :] = jnp.where(pos == 0, 0.0, prev)
    buf_ref[CONV_HALO:CONV_HALO + tt, :] = main_ref[...].astype(F32)
    buf_ref[CONV_HALO + tt:, :] = jnp.where(pos == tiles_per_seq - 1, 0.0, nxt)
    col = lax.broadcasted_iota(jnp.int32, o_ref.shape, 0) % GRID_W
    acc = None
    for dc in range(3):
        part = None
        for dr in range(3):
            off = (dr - 1) * GRID_W + (dc - 1)
            term = buf_ref[CONV_HALO + off:CONV_HALO + off + tt, :] * w_ref[dr * 3 + dc:dr * 3 + dc + 1, :]
            part = term if part is None else part + term
        if on_grid and dc == 0:
            part = jnp.where(col == 0, 0.0, part)
        if on_grid and dc == 2:
            part = jnp.where(col == GRID_W - 1, 0.0, part)
        acc = part if acc is None else acc + part
    o_ref[...] = acc.astype(o_ref.dtype)


def _conv(p, w9, seq_len, on_grid, tt):
    rows = p.shape[0]
    ct = 512
    width = 3 * A_WIDTH
    tiles_per_seq = seq_len // tt
    hb = tt // CONV_HALO
    last_hb = rows // CONV_HALO - 1
    kern = functools.partial(_conv_kernel, tiles_per_seq=tiles_per_seq, on_grid=on_grid)
    return pl.pallas_call(
        kern,
        grid=(rows // tt, width // ct),
        in_specs=[pl.BlockSpec((tt, ct), lambda i, j: (i, j)),
                  pl.BlockSpec((CONV_HALO, ct), lambda i, j: (jnp.maximum(i * hb - 1, 0), j)),
                  pl.BlockSpec((CONV_HALO, ct), lambda i, j: (jnp.minimum((i + 1) * hb, last_hb), j)),
                  pl.BlockSpec((9, ct), lambda i, j: (0, j))],
        out_specs=pl.BlockSpec((tt, ct), lambda i, j: (i, j)),
        out_shape=jax.ShapeDtypeStruct((rows, width), BF16),
        scratch_shapes=[pltpu.VMEM((tt + 2 * CONV_HALO, ct), F32)],
        compiler_params=_cparams(("parallel", "parallel")),
    )(p, p, p, w9)


def _chunk_index(d, n, n_chunks):
    return n + d * (n_chunks - 1 - 2 * n)


def _rwkv_features(d, out, rkv_ref, lora_ref, aw2_ref, aa2_ref, aw0_ref, aa0_ref, kk_ref, ka_ref, hsum_ref,
                   hexp_ref):
    r = rkv_ref[:, :A_WIDTH].astype(F32)
    k = rkv_ref[:, A_WIDTH:2 * A_WIDTH].astype(F32)
    lora = lora_ref[...]
    lw = jnp.tanh(lora[:, :LANES].astype(F32))
    w_log = aw0_ref[d] + _dot(lw, aw2_ref[d])
    iclr_pre = aa0_ref[d] + _dot(lora[:, LANES:], aa2_ref[d])
    kk = k * kk_ref[...]
    ss = _dot(kk * kk, hsum_ref[...])
    yield
    logw = -math.exp(-0.5) * _sigmoid(w_log)
    iclr = _sigmoid(iclr_pre)
    ss = _dot(ss, hexp_ref[...])
    row = lax.broadcasted_iota(jnp.int32, (CHUNK, CHUNK), 0)
    col = lax.broadcasted_iota(jnp.int32, (CHUNK, CHUNK), 1)
    tri = jnp.where((col <= row) if d == 0 else (col >= row), 1.0, 0.0).astype(BF16)
    cum = _dot_exact_lhs(tri, logw)
    tot = jnp.sum(logw, axis=0, keepdims=True)
    yield
    kk = kk * lax.rsqrt(ss + 1e-12)
    kdir = k * (1.0 + (iclr - 1.0) * ka_ref[...])
    b = kk * iclr
    yield
    e_neg = jnp.exp(-cum)
    out["a_t"] = -kk * jnp.exp(cum - logw)
    out["r_t"] = r * jnp.exp(cum)
    yield
    out["b_t"] = b * e_neg
    out["k_t"] = kdir * e_neg
    yield
    e_rel = jnp.exp(tot - cum)
    out["b_h"] = b * e_rel
    out["k_h"] = kdir * e_rel
    out["p_tot"] = jnp.exp(tot)
    out["v"] = rkv_ref[:, 2 * A_WIDTH:].astype(F32)


def _rwkv_pairs(d, f, y_ref, st_ref, filler=()):
    filler = iter(filler)
    tick = lambda: next(filler, None)
    prow = lax.broadcasted_iota(jnp.int32, (CHUNK, LANES), 0)
    pcol = lax.broadcasted_iota(jnp.int32, (CHUNK, LANES), 1) % CHUNK
    before = (pcol < prow) if d == 0 else (pcol > prow)
    upto = (pcol <= prow) if d == 0 else (pcol >= prow)
    brow = lax.broadcasted_iota(jnp.int32, (LANES, LANES), 0)
    bcol = lax.broadcasted_iota(jnp.int32, (LANES, LANES), 1)
    same_head = (brow // CHUNK) == (bcol // CHUNK)
    eye = brow == bcol

    def bd(x):
        return jnp.where(same_head, jnp.concatenate([x, x], axis=0), 0.0)

    a_t, r_t, b_t, k_t, b_h, k_h, v, p_tot = (f[name] for name in
                                               ("a_t", "r_t", "b_t", "k_t", "b_h", "k_h", "v", "p_tot"))
    sls = [slice(p * LANES, (p + 1) * LANES) for p in range(A_PAIRS)]
    each = lambda fn, *lists: [fn(*xs) for xs in zip(*lists)]
    mm = [_dot_nt(jnp.concatenate([a_t[:, sl], r_t[:, sl]], axis=0),
                  jnp.concatenate([bd(b_t[:, sl]), bd(k_t[:, sl])], axis=0)) for sl in sls]
    n_bd = [bd(jnp.where(before, m[:CHUNK, :LANES], 0.0)) for m in mm]
    mak_bd = [bd(jnp.where(before, m[:CHUNK, LANES:], 0.0)) for m in mm]
    qb_bd = [bd(jnp.where(upto, m[CHUNK:, :LANES], 0.0)) for m in mm]
    qk_bd = [bd(jnp.where(upto, m[CHUNK:, LANES:], 0.0)) for m in mm]
    v_bd = [bd(v[:, sl]) for sl in sls]
    tick()
    mq = each(lambda mk, qk, vb: _dot(jnp.concatenate([mk, qk], axis=0), vb), mak_bd, qk_bd, v_bd)
    tick()

    t_inv = [jnp.where(eye, 1.0, 0.0) + nb for nb in n_bd]
    x = [_dot(nb, nb) for nb in n_bd]
    for _ in range(4):
        z = each(lambda xx, tt: _dot(xx, jnp.concatenate([xx, tt], axis=1)), x, t_inv)
        x = [zz[:, :LANES] for zz in z]
        t_inv = each(lambda tt, zz: tt + zz[:, LANES:], t_inv, z)
        tick()
    t_inv = each(lambda tt, xx: tt + _dot(xx, tt), t_inv, x)
    for _ in filler:
        pass

    wu = each(lambda tt, sl, m: _dot(tt, jnp.concatenate([bd(a_t[:, sl]), m[:LANES]], axis=1)),
              t_inv, sls, mq)
    ry = each(lambda qb, w, sl, m: _dot(qb, w) + jnp.concatenate([bd(r_t[:, sl]), m[LANES:]], axis=1),
              qb_bd, wu, sls, mq)
    zeros = jnp.zeros((LANES, LANES), F32)
    gh = each(lambda sl, w, vb: _dot(jnp.concatenate([bd(b_h[:, sl]), bd(k_h[:, sl])], axis=0).T,
                                     jnp.concatenate([w, jnp.concatenate([zeros, vb], axis=1)], axis=0)),
              sls, wu, v_bd)
    for p, (sl, ry_p, gh_p) in enumerate(zip(sls, ry, gh)):
        g = gh_p[:, :LANES] + jnp.where(eye, p_tot[:, sl], 0.0)
        rg = _dot(jnp.concatenate([ry_p[:, :LANES], g], axis=0), st_ref[d, p])
        y = rg[:LANES] + ry_p[:, LANES:]
        y_ref[0, :, sl] = (y[:CHUNK] + y[CHUNK:]).astype(y_ref.dtype)
        st_ref[d, p] = rg[LANES:] + gh_p[:, LANES:]


def _gla_stages(qkv_refs, lg_refs, w2_ref, bg_ref, o_refs, st_ref):
    dirs = (0, 1)
    row = lax.broadcasted_iota(jnp.int32, (CHUNK, CHUNK), 0)
    col = lax.broadcasted_iota(jnp.int32, (CHUNK, CHUNK), 1)
    upto = [col <= row, col >= row]
    z = [_dot(lg_refs[d][...], w2_ref[d]) + bg_ref[d] for d in dirs]
    yield
    gk = [(jnp.minimum(zz, 0.0) - jnp.log1p(jnp.exp(-jnp.abs(zz)))) * (1.0 / B_GATE_NORM) for zz in z]
    cum = [_dot_exact_lhs(jnp.where(upto[d], 1.0, 0.0).astype(BF16), gk[d]) for d in dirs]
    tot = [jnp.sum(g, axis=0, keepdims=True) for g in gk]
    yield
    q = [ref[:, :B_KEY].astype(F32) for ref in qkv_refs]
    k = [ref[:, B_KEY:2 * B_KEY].astype(F32) for ref in qkv_refs]
    q_dec = [(q[d] * (B_DK ** -0.5) * jnp.exp(cum[d])).astype(BF16) for d in dirs]
    k_inv = [(k[d] * jnp.exp(-cum[d])).astype(BF16) for d in dirs]
    k_state = [(k[d] * jnp.exp(tot[d] - cum[d])).astype(BF16) for d in dirs]
    decay = [jnp.exp(t) for t in tot]
    yield
    chains = [(d, h) for d in dirs for h in range(B_HEADS)]
    ks = lambda h: slice(h * B_DK, (h + 1) * B_DK)
    vs = lambda h: slice(h * B_DV, (h + 1) * B_DV)
    v = {(d, h): qkv_refs[d][:, 2 * B_KEY + h * B_DV:2 * B_KEY + (h + 1) * B_DV] for d, h in chains}
    att = [jnp.where(upto[d], _dot_nt(q_dec[d][:, ks(h)], k_inv[d][:, ks(h)]), 0.0) for d, h in chains]
    yield
    s_t = [st_ref[d, h] for d, h in chains]
    outs = [_dot_nt(q_dec[d][:, ks(h)], s) + _dot(a, v[d, h]) for (d, h), s, a in zip(chains, s_t, att)]
    for (d, h), o in zip(chains, outs):
        o_refs[d][0, :, vs(h)] = o.astype(o_refs[d].dtype)
    yield
    for (d, h), s in zip(chains, s_t):
        st_ref[d, h] = s * decay[d][:, ks(h)] + _dot(v[d, h].astype(F32).T, k_state[d][:, ks(h)])


def _scan_kernel(rkvf_ref, loraf_ref, qkvf_ref, lgf_ref, rkvr_ref, lorar_ref, qkvr_ref, lgr_ref,
                 aw2_ref, aa2_ref, aw0_ref, aa0_ref, kk_ref, ka_ref, hsum_ref, hexp_ref, gw2_ref, gb_ref,
                 sa0_ref, sb0_ref, yf_ref, yr_ref, of_ref, or_ref, safin_ref, sbfin_ref, sta_ref, stb_ref):
    n = pl.program_id(1)

    @pl.when(n == 0)
    def _():
        sta_ref[...] = sa0_ref[:, 0]
        stb_ref[...] = sb0_ref[:, 0]

    wrefs = (aw2_ref, aa2_ref, aw0_ref, aa0_ref, kk_ref, ka_ref, hsum_ref, hexp_ref)
    feat_f, feat_r = {}, {}
    for _ in _rwkv_features(0, feat_f, rkvf_ref, loraf_ref, *wrefs):
        pass
    _rwkv_pairs(0, feat_f, yf_ref, sta_ref, filler=_rwkv_features(1, feat_r, rkvr_ref, lorar_ref, *wrefs))
    _rwkv_pairs(1, feat_r, yr_ref, sta_ref,
                filler=_gla_stages((qkvf_ref, qkvr_ref), (lgf_ref, lgr_ref), gw2_ref, gb_ref, (of_ref, or_ref),
                                   stb_ref))

    @pl.when(n == pl.num_programs(1) - 1)
    def _():
        safin_ref[:, 0] = sta_ref[...]
        sbfin_ref[:, 0] = stb_ref[...]


def _scans(rkv_c, p, wts, sa0, sb0, bsz, seq_len):
    n_chunks = seq_len // CHUNK
    qkv_w = 2 * B_KEY + B_WIDTH

    def tok_specs(d):
        row = lambda b, n: b * n_chunks + _chunk_index(d, n, n_chunks)
        return [pl.BlockSpec((CHUNK, 3 * A_WIDTH), lambda b, n: (row(b, n), 0)),
                pl.BlockSpec((CHUNK, 2 * LANES), lambda b, n: (row(b, n), C_LWA // (2 * LANES))),
                pl.BlockSpec((CHUNK, qkv_w), lambda b, n: (row(b, n), C_QK // qkv_w)),
                pl.BlockSpec((CHUNK, LANES), lambda b, n: (row(b, n), C_LG // LANES))]

    const = lambda shape: pl.BlockSpec(shape, lambda b, n: (0,) * len(shape))
    sa_spec = pl.BlockSpec((2, 1, A_PAIRS, LANES, LANES), lambda b, n: (0, b, 0, 0, 0))
    sb_spec = pl.BlockSpec((2, 1, B_HEADS, B_DV, B_DK), lambda b, n: (0, b, 0, 0, 0))
    sa_shape = jax.ShapeDtypeStruct((2, bsz, A_PAIRS, LANES, LANES), F32)
    sb_shape = jax.ShapeDtypeStruct((2, bsz, B_HEADS, B_DV, B_DK), F32)
    out_spec = lambda d: pl.BlockSpec((1, CHUNK, A_WIDTH), lambda b, n: (b, _chunk_index(d, n, n_chunks), 0))
    out_shape = jax.ShapeDtypeStruct((bsz, seq_len, A_WIDTH), F32)
    assert A_WIDTH == B_WIDTH
    return pl.pallas_call(
        _scan_kernel,
        grid=(bsz, n_chunks),
        in_specs=tok_specs(0) + tok_specs(1) + [
            const((2, LANES, A_WIDTH)), const((2, LANES, A_WIDTH)), const((2, 1, A_WIDTH)),
            const((2, 1, A_WIDTH)), const((1, A_WIDTH)), const((1, A_WIDTH)), const((A_WIDTH, LANES)),
            const((LANES, A_WIDTH)), const((2, LANES, B_KEY)), const((2, 1, B_KEY)), sa_spec, sb_spec],
        out_specs=[out_spec(0), out_spec(1), out_spec(0), out_spec(1), sa_spec, sb_spec],
        out_shape=[out_shape, out_shape, out_shape, out_shape, sa_shape, sb_shape],
        scratch_shapes=[pltpu.VMEM((2, A_PAIRS, LANES, LANES), F32), pltpu.VMEM((2, B_HEADS, B_DV, B_DK), F32)],
        compiler_params=_cparams(("parallel", "arbitrary")),
    )(rkv_c, p, p, p, rkv_c, p, p, p, wts["aw2"], wts["aa2"], wts["aw0"], wts["aa0"], wts["a_k_k"],
      wts["a_k_a"], wts["hsum_a"], wts["hexp_a"], wts["gw2"], wts["gb"], sa0, sb0)


def _merge_kernel(x_ref, gate_ref, yaf_ref, yar_ref, obf_ref, obr_ref, r_ref, k_ref, v_ref, la_ref, za_ref,
                  zb_ref, ga_ref, gb_ref, aa2_ref, aa0_ref, ka_ref, rk_ref, lng_ref, lnb_ref, bng_ref, hsa_ref,
                  hea_ref, hsb_ref, heb_ref, wa_ref, wb_ref, wo_ref, fg_ref, o_ref):
    wkv = yaf_ref[0] + yar_ref[0]
    mu = _dot(_dot(wkv, hsa_ref[...]), hea_ref[...]) * (1.0 / A_HEAD)
    cen = wkv - mu
    var = _dot(_dot(cen * cen, hsa_ref[...]), hea_ref[...]) * (1.0 / A_HEAD)
    y_a = cen * lax.rsqrt(var + A_LN_EPS) * lng_ref[...] + lnb_ref[...]
    r = r_ref[...].astype(F32)
    k = k_ref[...].astype(F32)
    v = v_ref[...].astype(F32)
    la = la_ref[...][:, LANES:]
    iclr_sum = (_sigmoid(aa0_ref[0] + _dot(la, aa2_ref[0])) + _sigmoid(aa0_ref[1] + _dot(la, aa2_ref[1])))
    rk = r * k * rk_ref[...] * (2.0 + (iclr_sum - 2.0) * ka_ref[...])
    bonus = _dot(_dot(rk, hsa_ref[...]), hea_ref[...])
    y_a = (y_a + bonus * v) * _silu(za_ref[...].astype(F32))
    ob = obf_ref[0] + obr_ref[0]
    ms = _dot(_dot(ob * ob, hsb_ref[...]), heb_ref[...]) * (1.0 / B_DV)
    y_b = ob * lax.rsqrt(ms + EPS) * bng_ref[...] * _silu(zb_ref[...].astype(F32))
    mixed = (_sigmoid(ga_ref[...].astype(F32)) * _dot(y_a, wa_ref[...])
             + _sigmoid(gb_ref[...].astype(F32)) * _dot(y_b, wb_ref[...]))
    y = _dot(mixed, wo_ref[...])
    xo = x_ref[...] + gate_ref[0] * y
    o_ref[...] = xo * lax.rsqrt(jnp.mean(xo * xo, axis=-1, keepdims=True) + EPS) * fg_ref[...]


def _merge(x2d, gate, y_af, y_ar, o_bf, o_br, rkv_c, p, wts, bsz, seq_len, tm):
    rows, d = x2d.shape
    tiles_per_seq = seq_len // tm
    tokc = lambda cblk: (lambda i: (i, cblk))
    scan_spec = pl.BlockSpec((1, tm, A_WIDTH), lambda i: (i // tiles_per_seq, i % tiles_per_seq, 0))
    const = lambda shape: pl.BlockSpec(shape, lambda i: (0,) * len(shape), pipeline_mode=pl.Buffered(1))
    return pl.pallas_call(
        _merge_kernel,
        grid=(rows // tm,),
        in_specs=[pl.BlockSpec((tm, d), lambda i: (i, 0)),
                  pl.BlockSpec((1, 1, d), lambda i: (i // tiles_per_seq, 0, 0)),
                  scan_spec, scan_spec, scan_spec, scan_spec,
                  pl.BlockSpec((tm, A_WIDTH), tokc(0)),
                  pl.BlockSpec((tm, A_WIDTH), tokc(1)),
                  pl.BlockSpec((tm, A_WIDTH), tokc(2)),
                  pl.BlockSpec((tm, 2 * LANES), tokc(C_LWA // (2 * LANES))),
                  pl.BlockSpec((tm, A_WIDTH), tokc(C_ZA // A_WIDTH)),
                  pl.BlockSpec((tm, B_WIDTH), tokc(C_ZB // B_WIDTH)),
                  pl.BlockSpec((tm, d), tokc(C_GA // d)),
                  pl.BlockSpec((tm, d), tokc(C_GB // d)),
                  const((2, LANES, A_WIDTH)), const((2, 1, A_WIDTH)), const((1, A_WIDTH)), const((1, A_WIDTH)),
                  const((1, A_WIDTH)), const((1, A_WIDTH)), const((1, B_WIDTH)),
                  const((A_WIDTH, LANES)), const((LANES, A_WIDTH)), const((B_WIDTH, LANES)),
                  const((LANES, B_WIDTH)),
                  const((A_WIDTH, d)), const((B_WIDTH, d)), const((d, d)), const((1, d))],
        out_specs=pl.BlockSpec((tm, d), lambda i: (i, 0)),
        out_shape=jax.ShapeDtypeStruct((rows, d), F32),
        compiler_params=_cparams(("parallel",)),
    )(x2d, gate, y_af, y_ar, o_bf, o_br, rkv_c, rkv_c, rkv_c, p, p, p, p, p,
      wts["aa2"], wts["aa0"], wts["a_k_a"], wts["a_r_k"], wts["lnx_g"], wts["lnx_b"], wts["bng"],
      wts["hsum_a"], wts["hexp_a"], wts["hsum_b"], wts["hexp_b"], wts["w_a"], wts["w_b"], wts["w_out"],
      wts["final_g"])


def _head_indicator(width, head):
    m = (jnp.arange(width)[:, None] // head == jnp.arange(LANES)[None, :]).astype(BF16)
    return m, m.T


def _prep_weights(w_in, conv_w, a_w0, a_w2, a_a0, a_a2, a_k_k, a_k_a, a_r_k, a_lnx_g, a_lnx_b, b_gk_w2,
                  b_gk_b, b_norm_g, w_a, w_b, w_out, final_g):
    sizes = (3 * A_WIDTH, A_WIDTH, 2 * A_RANK, 2 * A_RANK, B_KEY, B_KEY, B_WIDTH, B_WIDTH, 2 * B_GATE_RANK,
             D_MODEL, D_MODEL)
    offs = [0]
    for s in sizes:
        offs.append(offs[-1] + s)
    seg = lambda i: w_in[:, offs[i]:offs[i + 1]]
    rkv, z_a, lora_w, lora_a, q_b, k_b, v_b, z_b, lora_g, g_a, g_b = (seg(i) for i in range(11))
    pad_g = jnp.zeros((D_MODEL, LANES - 2 * B_GATE_RANK), w_in.dtype)
    pad_end = jnp.zeros((D_MODEL, P_PAD - C_LG - LANES), w_in.dtype)
    w_perm = jnp.concatenate([rkv, z_a, g_a, g_b, q_b, k_b, v_b, z_b, lora_w, lora_a, lora_g, pad_g, pad_end],
                             axis=1).astype(BF16)

    def dir_pad(w2, rank):
        out = jnp.zeros((2, LANES, w2.shape[-1]), F32)
        for d in range(2):
            out = out.at[d, d * rank:(d + 1) * rank].set(w2[d])
        return out.astype(BF16)

    hsum_a, hexp_a = _head_indicator(A_WIDTH, A_HEAD)
    hsum_b, hexp_b = _head_indicator(B_WIDTH, B_DV)
    return {
        "w_perm": w_perm,
        "conv9": conv_w.reshape(9, 3 * A_WIDTH),
        "aw2": dir_pad(a_w2, A_RANK), "aa2": dir_pad(a_a2, A_RANK),
        "aw0": a_w0.reshape(2, 1, A_WIDTH), "aa0": a_a0.reshape(2, 1, A_WIDTH),
        "a_k_k": a_k_k.reshape(1, A_WIDTH), "a_k_a": a_k_a.reshape(1, A_WIDTH),
        "a_r_k": a_r_k.reshape(1, A_WIDTH),
        "lnx_g": a_lnx_g.reshape(1, A_WIDTH), "lnx_b": a_lnx_b.reshape(1, A_WIDTH),
        "gw2": dir_pad(b_gk_w2, B_GATE_RANK), "gb": b_gk_b.reshape(2, 1, B_KEY),
        "bng": jnp.tile(b_norm_g, B_HEADS).reshape(1, B_WIDTH),
        "hsum_a": hsum_a, "hexp_a": hexp_a, "hsum_b": hsum_b, "hexp_b": hexp_b,
        "w_a": w_a.astype(BF16), "w_b": w_b.astype(BF16), "w_out": w_out.astype(BF16),
        "final_g": final_g.reshape(1, D_MODEL),
    }


def _pick_tile(n, cap):
    t = cap
    while n % t:
        t //= 2
    return t


def kernel(x, c, ctx, c_ctx, w_mod, b_mod, norm_g, w_in, conv_w, a_w0, a_w2, a_a0, a_a2, a_k_k, a_k_a,
           a_r_k, a_lnx_g, a_lnx_b, b_gk_w2, b_gk_b, b_norm_g, w_a, w_b, w_out, final_g):
    assert w_in.shape[0] == 1, "single layer"
    bsz, seq_len, d = x.shape
    ctx_len = ctx.shape[1]
    wts = _prep_weights(w_in[0], conv_w[0], a_w0[0], a_w2[0], a_a0[0], a_a2[0], a_k_k[0], a_k_a[0], a_r_k[0],
                        a_lnx_g[0], a_lnx_b[0], b_gk_w2[0], b_gk_b[0], b_norm_g[0], w_a[0], w_b[0], w_out[0],
                        final_g)

    cond = jnp.concatenate([c, c_ctx[None], jnp.zeros((8 - bsz - 1, d), F32)], axis=0)
    mod = _modulation(cond, w_mod[0], b_mod[0])
    shift, scale, gate = mod[:, :d], mod[:, d:2 * d], mod[:, 2 * d:]
    as_rows = lambda m, lo, hi: m[lo:hi].reshape(hi - lo, 1, d)

    ctx2d = ctx.reshape(bsz * ctx_len, d)
    p_ctx = _inproj(ctx2d, as_rows(scale, bsz, bsz + 1), as_rows(shift, bsz, bsz + 1), norm_g[0],
                    wts["w_perm"], bsz * ctx_len, _pick_tile(bsz * ctx_len, 1024))
    conv_ctx = wts["conv9"] * jnp.array([0.0, 1.0, 0.0], F32).repeat(3)[:, None]
    rkv_ctx = _conv(p_ctx, conv_ctx, ctx_len, False, _pick_tile(ctx_len, 512))
    zero_a = jnp.zeros((2, bsz, A_PAIRS, LANES, LANES), F32)
    zero_b = jnp.zeros((2, bsz, B_HEADS, B_DV, B_DK), F32)
    *_, sa_ctx, sb_ctx = _scans(rkv_ctx, p_ctx, wts, zero_a, zero_b, bsz, ctx_len)

    x2d = x.reshape(bsz * seq_len, d)
    p = _inproj(x2d, as_rows(scale, 0, bsz), as_rows(shift, 0, bsz), norm_g[0], wts["w_perm"], seq_len,
                _pick_tile(seq_len, 1024))
    rkv_c = _conv(p, wts["conv9"], seq_len, True, _pick_tile(seq_len, 512))
    y_af, y_ar, o_bf, o_br, _, _ = _scans(rkv_c, p, wts, sa_ctx, sb_ctx, bsz, seq_len)
    out = _merge(x2d, as_rows(gate, 0, bsz), y_af, y_ar, o_bf, o_br, rkv_c, p, wts, bsz, seq_len,
                 _pick_tile(seq_len, 256))
    return out.reshape(bsz, seq_len, d)
```

```python
import functools
import math

import jax
import jax.numpy as jnp
from jax import lax
from jax.experimental import pallas as pl
from jax.experimental.pallas import tpu as pltpu

F32 = jnp.float32
BF16 = jnp.bfloat16

D_MODEL = 2048
GRID_W = 64
EPS = 1e-6

A_WIDTH = 1024
A_HEAD = 64
A_HEADS = 16
A_PAIRS = A_HEADS // 2
A_RANK = 64
A_LN_EPS = 64e-5

B_WIDTH = 1024
B_HEADS = 4
B_KEY = 512
B_DK = 128
B_DV = 256
B_GATE_RANK = 16
B_GATE_NORM = 16.0

CHUNK = 64
LANES = 128

C_RKV = 0
C_ZA = 3072
C_GA = 4096
C_GB = 6144
C_QK = 8192
C_VB = 9216
C_ZB = 10240
C_LWA = 11264
C_LG = 11520
P_PAD = 11776

VMEM_LIMIT = 56 * 1024 * 1024


def _cparams(sem):
    return pltpu.CompilerParams(dimension_semantics=sem, vmem_limit_bytes=VMEM_LIMIT)


def _dot(a, b):
    return jnp.dot(a.astype(BF16), b.astype(BF16), preferred_element_type=F32)


def _dot_nt(a, b):
    return lax.dot_general(a.astype(BF16), b.astype(BF16), (((1,), (1,)), ((), ())),
                           preferred_element_type=F32)


def _split(a):
    hi = a.astype(BF16)
    lo = (a - hi.astype(F32)).astype(BF16)
    return hi, lo


def _dot_exact_lhs(a_bf16, b):
    bh, bl = _split(b)
    d = functools.partial(jnp.dot, preferred_element_type=F32)
    return d(a_bf16, bh) + d(a_bf16, bl)


def _sigmoid(x):
    return 1.0 / (1.0 + jnp.exp(-x))


def _silu(x):
    return x * _sigmoid(x)


def _mod_kernel(c_ref, w_ref, b_ref, o_ref):
    o_ref[...] = _dot(_silu(c_ref[...]), w_ref[...]) + b_ref[...]


def _modulation(cond, w_mod, b_mod):
    rows, d = cond.shape
    n = w_mod.shape[1]
    tn = 1024
    return pl.pallas_call(
        _mod_kernel,
        grid=(n // tn,),
        in_specs=[pl.BlockSpec((rows, d), lambda j: (0, 0)),
                  pl.BlockSpec((d, tn), lambda j: (0, j)),
                  pl.BlockSpec((1, tn), lambda j: (0, j))],
        out_specs=pl.BlockSpec((rows, tn), lambda j: (0, j)),
        out_shape=jax.ShapeDtypeStruct((rows, n), F32),
        compiler_params=_cparams(("parallel",)),
    )(cond, w_mod, b_mod.reshape(1, n))


IN_TILE = 512
IN_PIECES = (C_GA - C_RKV, C_QK - C_GA, C_LWA - C_QK, P_PAD - C_LWA)


def _inproj_kernel(x_ref, sc_ref, sh_ref, g_ref, *rest):
    w_refs, o_ref, h_ref = rest[:len(IN_PIECES)], rest[-2], rest[-1]
    j = pl.program_id(1)

    @pl.when(j == 0)
    def _():
        x = x_ref[...]
        y = x * lax.rsqrt(jnp.mean(x * x, axis=-1, keepdims=True) + EPS)
        h = (y * g_ref[...]) * (1.0 + sc_ref[0]) + sh_ref[0]
        h_ref[...] = h.astype(BF16)

    lo = 0
    for w_ref, width in zip(w_refs, IN_PIECES):
        hi = lo + width // IN_TILE

        @pl.when(jnp.logical_and(j >= lo, j < hi))
        def _(w_ref=w_ref):
            o_ref[...] = jnp.dot(h_ref[...], w_ref[...], preferred_element_type=F32).astype(o_ref.dtype)

        lo = hi


def _inproj(x2d, scale, shift, norm_g, w_pieces, rows_per_mod, tm):
    rows, d = x2d.shape
    tiles_per_mod = rows_per_mod // tm
    w_specs, lo = [], 0
    for width in IN_PIECES:
        n = width // IN_TILE
        w_specs.append(pl.BlockSpec((d, IN_TILE), lambda i, j, lo=lo, n=n: (0, jnp.clip(j - lo, 0, n - 1))))
        lo += n
    return pl.pallas_call(
        _inproj_kernel,
        grid=(rows // tm, P_PAD // IN_TILE),
        in_specs=[pl.BlockSpec((tm, d), lambda i, j: (i, 0)),
                  pl.BlockSpec((1, 1, d), lambda i, j: (i // tiles_per_mod, 0, 0)),
                  pl.BlockSpec((1, 1, d), lambda i, j: (i // tiles_per_mod, 0, 0)),
                  pl.BlockSpec((1, d), lambda i, j: (0, 0))] + w_specs,
        out_specs=pl.BlockSpec((tm, IN_TILE), lambda i, j: (i, j)),
        out_shape=jax.ShapeDtypeStruct((rows, P_PAD), BF16),
        scratch_shapes=[pltpu.VMEM((tm, d), BF16)],
        compiler_params=_cparams(("parallel", "arbitrary")),
    )(x2d, scale, shift, norm_g.reshape(1, d), *w_pieces)


CONV_HALO = 128


def _conv_kernel(main_ref, prev_ref, next_ref, w_ref, o_ref, buf_ref, *, tiles_per_seq, on_grid):
    tt = main_ref.shape[0]
    pos = pl.program_id(0) % tiles_per_seq
    prev = prev_ref[...].astype(F32)
    nxt = next_ref[...].astype(F32)
    buf_ref[0:CONV_HALO, :] = jnp.where(pos == 0, 0.0, prev)
    buf_ref[CONV_HALO:CONV_HALO + tt, :] = main_ref[...].astype(F32)
    buf_ref[CONV_HALO + tt:, :] = jnp.where(pos == tiles_per_seq - 1, 0.0, nxt)
    col = lax.broadcasted_iota(jnp.int32, o_ref.shape, 0) % GRID_W
    acc = None
    for dc in range(3):
        part = None
        for dr in range(3):
            off = (dr - 1) * GRID_W + (dc - 1)
            term = buf_ref[CONV_HALO + off:CONV_HALO + off + tt, :] * w_ref[dr * 3 + dc:dr * 3 + dc + 1, :]
            part = term if part is None else part + term
        if on_grid and dc == 0:
            part = jnp.where(col == 0, 0.0, part)
        if on_grid and dc == 2:
            part = jnp.where(col == GRID_W - 1, 0.0, part)
        acc = part if acc is None else acc + part
    o_ref[...] = acc.astype(o_ref.dtype)


def _conv(p, w9, seq_len, on_grid, tt):
    rows = p.shape[0]
    ct = 512
    width = 3 * A_WIDTH
    tiles_per_seq = seq_len // tt
    hb = tt // CONV_HALO
    last_hb = rows // CONV_HALO - 1
    kern = functools.partial(_conv_kernel, tiles_per_seq=tiles_per_seq, on_grid=on_grid)
    return pl.pallas_call(
        kern,
        grid=(rows // tt, width // ct),
        in_specs=[pl.BlockSpec((tt, ct), lambda i, j: (i, j)),
                  pl.BlockSpec((CONV_HALO, ct), lambda i, j: (jnp.maximum(i * hb - 1, 0), j)),
                  pl.BlockSpec((CONV_HALO, ct), lambda i, j: (jnp.minimum((i + 1) * hb, last_hb), j)),
                  pl.BlockSpec((9, ct), lambda i, j: (0, j))],
        out_specs=pl.BlockSpec((tt, ct), lambda i, j: (i, j)),
        out_shape=jax.ShapeDtypeStruct((rows, width), BF16),
        scratch_shapes=[pltpu.VMEM((tt + 2 * CONV_HALO, ct), F32)],
        compiler_params=_cparams(("parallel", "parallel")),
    )(p, p, p, w9)


def _chunk_index(d, n, n_chunks):
    return n + d * (n_chunks - 1 - 2 * n)


def _rwkv_features(d, out, rkv_ref, lora_ref, aw2_ref, aa2_ref, aw0_ref, aa0_ref, kk_ref, ka_ref, hsum_ref,
                   hexp_ref):
    r = rkv_ref[:, :A_WIDTH].astype(F32)
    k = rkv_ref[:, A_WIDTH:2 * A_WIDTH].astype(F32)
    lora = lora_ref[...]
    lw = jnp.tanh(lora[:, :LANES].astype(F32))
    w_log = aw0_ref[d] + _dot(lw, aw2_ref[d])
    iclr_pre = aa0_ref[d] + _dot(lora[:, LANES:], aa2_ref[d])
    kk = k * kk_ref[...]
    ss = _dot(kk * kk, hsum_ref[...])
    yield
    logw = -math.exp(-0.5) * _sigmoid(w_log)
    iclr = _sigmoid(iclr_pre)
    ss = _dot(ss, hexp_ref[...])
    row = lax.broadcasted_iota(jnp.int32, (CHUNK, CHUNK), 0)
    col = lax.broadcasted_iota(jnp.int32, (CHUNK, CHUNK), 1)
    tri = jnp.where((col <= row) if d == 0 else (col >= row), 1.0, 0.0).astype(BF16)
    cum = _dot_exact_lhs(tri, logw)
    tot = jnp.sum(logw, axis=0, keepdims=True)
    yield
    kk = kk * lax.rsqrt(ss + 1e-12)
    kdir = k * (1.0 + (iclr - 1.0) * ka_ref[...])
    b = kk * iclr
    yield
    e_neg = jnp.exp(-cum)
    out["a_t"] = -kk * jnp.exp(cum - logw)
    out["r_t"] = r * jnp.exp(cum)
    yield
    out["b_t"] = b * e_neg
    out["k_t"] = kdir * e_neg
    yield
    e_rel = jnp.exp(tot - cum)
    out["b_h"] = b * e_rel
    out["k_h"] = kdir * e_rel
    out["p_tot"] = jnp.exp(tot)
    out["v"] = rkv_ref[:, 2 * A_WIDTH:].astype(F32)


def _rwkv_pairs(d, f, y_ref, st_ref, filler=()):
    filler = iter(filler)
    tick = lambda: next(filler, None)
    prow = lax.broadcasted_iota(jnp.int32, (CHUNK, LANES), 0)
    pcol = lax.broadcasted_iota(jnp.int32, (CHUNK, LANES), 1) % CHUNK
    before = (pcol < prow) if d == 0 else (pcol > prow)
    upto = (pcol <= prow) if d == 0 else (pcol >= prow)
    brow = lax.broadcasted_iota(jnp.int32, (LANES, LANES), 0)
    bcol = lax.broadcasted_iota(jnp.int32, (LANES, LANES), 1)
    same_head = (brow // CHUNK) == (bcol // CHUNK)
    eye = brow == bcol

    def bd(x):
        return jnp.where(same_head, jnp.concatenate([x, x], axis=0), 0.0)

    a_t, r_t, b_t, k_t, b_h, k_h, v, p_tot = (f[name] for name in
                                               ("a_t", "r_t", "b_t", "k_t", "b_h", "k_h", "v", "p_tot"))
    sls = [slice(p * LANES, (p + 1) * LANES) for p in range(A_PAIRS)]
    each = lambda fn, *lists: [fn(*xs) for xs in zip(*lists)]
    mm = [_dot_nt(jnp.concatenate([a_t[:, sl], r_t[:, sl]], axis=0),
                  jnp.concatenate([bd(b_t[:, sl]), bd(k_t[:, sl])], axis=0)) for sl in sls]
    n_bd = [bd(jnp.where(before, m[:CHUNK, :LANES], 0.0)) for m in mm]
    mak_bd = [bd(jnp.where(before, m[:CHUNK, LANES:], 0.0)) for m in mm]
    qb_bd = [bd(jnp.where(upto, m[CHUNK:, :LANES], 0.0)) for m in mm]
    qk_bd = [bd(jnp.where(upto, m[CHUNK:, LANES:], 0.0)) for m in mm]
    v_bd = [bd(v[:, sl]) for sl in sls]
    tick()
    mq = each(lambda mk, qk, vb: _dot(jnp.concatenate([mk, qk], axis=0), vb), mak_bd, qk_bd, v_bd)
    tick()

    t_inv = [jnp.where(eye, 1.0, 0.0) + nb for nb in n_bd]
    x = [_dot(nb, nb) for nb in n_bd]
    for _ in range(4):
        z = each(lambda xx, tt: _dot(xx, jnp.concatenate([xx, tt], axis=1)), x, t_inv)
        x = [zz[:, :LANES] for zz in z]
        t_inv = each(lambda tt, zz: tt + zz[:, LANES:], t_inv, z)
        tick()
    t_inv = each(lambda tt, xx: tt + _dot(xx, tt), t_inv, x)
    for _ in filler:
        pass

    wu = each(lambda tt, sl, m: _dot(tt, jnp.concatenate([bd(a_t[:, sl]), m[:LANES]], axis=1)),
              t_inv, sls, mq)
    ry = each(lambda qb, w, sl, m: _dot(qb, w) + jnp.concatenate([bd(r_t[:, sl]), m[LANES:]], axis=1),
              qb_bd, wu, sls, mq)
    zeros = jnp.zeros((LANES, LANES), F32)
    gh = each(lambda sl, w, vb: _dot(jnp.concatenate([bd(b_h[:, sl]), bd(k_h[:, sl])], axis=0).T,
                                     jnp.concatenate([w, jnp.concatenate([zeros, vb], axis=1)], axis=0)),
              sls, wu, v_bd)
    for p, (sl, ry_p, gh_p) in enumerate(zip(sls, ry, gh)):
        g = gh_p[:, :LANES] + jnp.where(eye, p_tot[:, sl], 0.0)
        rg = _dot(jnp.concatenate([ry_p[:, :LANES], g], axis=0), st_ref[d, p])
        y = rg[:LANES] + ry_p[:, LANES:]
        y_ref[0, :, sl] = (y[:CHUNK] + y[CHUNK:]).astype(y_ref.dtype)
        st_ref[d, p] = rg[LANES:] + gh_p[:, LANES:]


def _gla_stages(qkv_refs, lg_refs, w2_ref, bg_ref, o_refs, st_ref):
    dirs = (0, 1)
    row = lax.broadcasted_iota(jnp.int32, (CHUNK, CHUNK), 0)
    col = lax.broadcasted_iota(jnp.int32, (CHUNK, CHUNK), 1)
    upto = [col <= row, col >= row]
    z = [_dot(lg_refs[d][...], w2_ref[d]) + bg_ref[d] for d in dirs]
    yield
    gk = [(jnp.minimum(zz, 0.0) - jnp.log1p(jnp.exp(-jnp.abs(zz)))) * (1.0 / B_GATE_NORM) for zz in z]
    cum = [_dot_exact_lhs(jnp.where(upto[d], 1.0, 0.0).astype(BF16), gk[d]) for d in dirs]
    tot = [jnp.sum(g, axis=0, keepdims=True) for g in gk]
    yield
    q = [ref[:, :B_KEY].astype(F32) for ref in qkv_refs]
    k = [ref[:, B_KEY:2 * B_KEY].astype(F32) for ref in qkv_refs]
    q_dec = [(q[d] * (B_DK ** -0.5) * jnp.exp(cum[d])).astype(BF16) for d in dirs]
    k_inv = [(k[d] * jnp.exp(-cum[d])).astype(BF16) for d in dirs]
    k_state = [(k[d] * jnp.exp(tot[d] - cum[d])).astype(BF16) for d in dirs]
    decay = [jnp.exp(t) for t in tot]
    yield
    chains = [(d, h) for d in dirs for h in range(B_HEADS)]
    ks = lambda h: slice(h * B_DK, (h + 1) * B_DK)
    vs = lambda h: slice(h * B_DV, (h + 1) * B_DV)
    v = {(d, h): qkv_refs[d][:, 2 * B_KEY + h * B_DV:2 * B_KEY + (h + 1) * B_DV] for d, h in chains}
    att = [jnp.where(upto[d], _dot_nt(q_dec[d][:, ks(h)], k_inv[d][:, ks(h)]), 0.0) for d, h in chains]
    yield
    s_t = [st_ref[d, h] for d, h in chains]
    outs = [_dot_nt(q_dec[d][:, ks(h)], s) + _dot(a, v[d, h]) for (d, h), s, a in zip(chains, s_t, att)]
    for (d, h), o in zip(chains, outs):
        o_refs[d][0, :, vs(h)] = o.astype(o_refs[d].dtype)
    yield
    for (d, h), s in zip(chains, s_t):
        st_ref[d, h] = s * decay[d][:, ks(h)] + _dot(v[d, h].astype(F32).T, k_state[d][:, ks(h)])


def _scan_kernel(rkvf_ref, loraf_ref, qkvf_ref, lgf_ref, rkvr_ref, lorar_ref, qkvr_ref, lgr_ref,
                 aw2_ref, aa2_ref, aw0_ref, aa0_ref, kk_ref, ka_ref, hsum_ref, hexp_ref, gw2_ref, gb_ref,
                 sa0_ref, sb0_ref, yf_ref, yr_ref, of_ref, or_ref, safin_ref, sbfin_ref, sta_ref, stb_ref):
    n = pl.program_id(1)

    @pl.when(n == 0)
    def _():
        sta_ref[...] = sa0_ref[:, 0]
        stb_ref[...] = sb0_ref[:, 0]

    wrefs = (aw2_ref, aa2_ref, aw0_ref, aa0_ref, kk_ref, ka_ref, hsum_ref, hexp_ref)
    feat_f, feat_r = {}, {}
    for _ in _rwkv_features(0, feat_f, rkvf_ref, loraf_ref, *wrefs):
        pass
    _rwkv_pairs(0, feat_f, yf_ref, sta_ref, filler=_rwkv_features(1, feat_r, rkvr_ref, lorar_ref, *wrefs))
    _rwkv_pairs(1, feat_r, yr_ref, sta_ref,
                filler=_gla_stages((qkvf_ref, qkvr_ref), (lgf_ref, lgr_ref), gw2_ref, gb_ref, (of_ref, or_ref),
                                   stb_ref))

    @pl.when(n == pl.num_programs(1) - 1)
    def _():
        safin_ref[:, 0] = sta_ref[...]
        sbfin_ref[:, 0] = stb_ref[...]


def _scans(rkv_c, p, wts, sa0, sb0, bsz, seq_len):
    n_chunks = seq_len // CHUNK
    qkv_w = 2 * B_KEY + B_WIDTH

    def tok_specs(d):
        row = lambda b, n: b * n_chunks + _chunk_index(d, n, n_chunks)
        return [pl.BlockSpec((CHUNK, 3 * A_WIDTH), lambda b, n: (row(b, n), 0)),
                pl.BlockSpec((CHUNK, 2 * LANES), lambda b, n: (row(b, n), C_LWA // (2 * LANES))),
                pl.BlockSpec((CHUNK, qkv_w), lambda b, n: (row(b, n), C_QK // qkv_w)),
                pl.BlockSpec((CHUNK, LANES), lambda b, n: (row(b, n), C_LG // LANES))]

    const = lambda shape: pl.BlockSpec(shape, lambda b, n: (0,) * len(shape))
    sa_spec = pl.BlockSpec((2, 1, A_PAIRS, LANES, LANES), lambda b, n: (0, b, 0, 0, 0))
    sb_spec = pl.BlockSpec((2, 1, B_HEADS, B_DV, B_DK), lambda b, n: (0, b, 0, 0, 0))
    sa_shape = jax.ShapeDtypeStruct((2, bsz, A_PAIRS, LANES, LANES), F32)
    sb_shape = jax.ShapeDtypeStruct((2, bsz, B_HEADS, B_DV, B_DK), F32)
    out_spec = lambda d: pl.BlockSpec((1, CHUNK, A_WIDTH), lambda b, n: (b, _chunk_index(d, n, n_chunks), 0))
    out_shape = jax.ShapeDtypeStruct((bsz, seq_len, A_WIDTH), F32)
    assert A_WIDTH == B_WIDTH
    return pl.pallas_call(
        _scan_kernel,
        grid=(bsz, n_chunks),
        in_specs=tok_specs(0) + tok_specs(1) + [
            const((2, LANES, A_WIDTH)), const((2, LANES, A_WIDTH)), const((2, 1, A_WIDTH)),
            const((2, 1, A_WIDTH)), const((1, A_WIDTH)), const((1, A_WIDTH)), const((A_WIDTH, LANES)),
            const((LANES, A_WIDTH)), const((2, LANES, B_KEY)), const((2, 1, B_KEY)), sa_spec, sb_spec],
        out_specs=[out_spec(0), out_spec(1), out_spec(0), out_spec(1), sa_spec, sb_spec],
        out_shape=[out_shape, out_shape, out_shape, out_shape, sa_shape, sb_shape],
        scratch_shapes=[pltpu.VMEM((2, A_PAIRS, LANES, LANES), F32), pltpu.VMEM((2, B_HEADS, B_DV, B_DK), F32)],
        compiler_params=_cparams(("parallel", "arbitrary")),
    )(rkv_c, p, p, p, rkv_c, p, p, p, wts["aw2"], wts["aa2"], wts["aw0"], wts["aa0"], wts["a_k_k"],
      wts["a_k_a"], wts["hsum_a"], wts["hexp_a"], wts["gw2"], wts["gb"], sa0, sb0)


def _merge_kernel(x_ref, gate_ref, yaf_ref, yar_ref, obf_ref, obr_ref, r_ref, k_ref, v_ref, la_ref, za_ref,
                  zb_ref, ga_ref, gb_ref, aa2_ref, aa0_ref, ka_ref, rk_ref, lng_ref, lnb_ref, bng_ref, hsa_ref,
                  hea_ref, hsb_ref, heb_ref, wa_ref, wb_ref, wo_ref, fg_ref, o_ref):
    wkv = yaf_ref[0] + yar_ref[0]
    mu = _dot(_dot(wkv, hsa_ref[...]), hea_ref[...]) * (1.0 / A_HEAD)
    cen = wkv - mu
    var = _dot(_dot(cen * cen, hsa_ref[...]), hea_ref[...]) * (1.0 / A_HEAD)
    y_a = cen * lax.rsqrt(var + A_LN_EPS) * lng_ref[...] + lnb_ref[...]
    r = r_ref[...].astype(F32)
    k = k_ref[...].astype(F32)
    v = v_ref[...].astype(F32)
    la = la_ref[...][:, LANES:]
    iclr_sum = (_sigmoid(aa0_ref[0] + _dot(la, aa2_ref[0])) + _sigmoid(aa0_ref[1] + _dot(la, aa2_ref[1])))
    rk = r * k * rk_ref[...] * (2.0 + (iclr_sum - 2.0) * ka_ref[...])
    bonus = _dot(_dot(rk, hsa_ref[...]), hea_ref[...])
    y_a = (y_a + bonus * v) * _silu(za_ref[...].astype(F32))
    ob = obf_ref[0] + obr_ref[0]
    ms = _dot(_dot(ob * ob, hsb_ref[...]), heb_ref[...]) * (1.0 / B_DV)
    y_b = ob * lax.rsqrt(ms + EPS) * bng_ref[...] * _silu(zb_ref[...].astype(F32))
    mixed = (_sigmoid(ga_ref[...].astype(F32)) * _dot(y_a, wa_ref[...])
             + _sigmoid(gb_ref[...].astype(F32)) * _dot(y_b, wb_ref[...]))
    y = _dot(mixed, wo_ref[...])
    xo = x_ref[...] + gate_ref[0] * y
    o_ref[...] = xo * lax.rsqrt(jnp.mean(xo * xo, axis=-1, keepdims=True) + EPS) * fg_ref[...]


def _merge(x2d, gate, y_af, y_ar, o_bf, o_br, rkv_c, p, wts, bsz, seq_len, tm):
    rows, d = x2d.shape
    tiles_per_seq = seq_len // tm
    tokc = lambda cblk: (lambda i: (i, cblk))
    scan_spec = pl.BlockSpec((1, tm, A_WIDTH), lambda i: (i // tiles_per_seq, i % tiles_per_seq, 0))
    const = lambda shape: pl.BlockSpec(shape, lambda i: (0,) * len(shape), pipeline_mode=pl.Buffered(1))
    return pl.pallas_call(
        _merge_kernel,
        grid=(rows // tm,),
        in_specs=[pl.BlockSpec((tm, d), lambda i: (i, 0)),
                  pl.BlockSpec((1, 1, d), lambda i: (i // tiles_per_seq, 0, 0)),
                  scan_spec, scan_spec, scan_spec, scan_spec,
                  pl.BlockSpec((tm, A_WIDTH), tokc(0)),
                  pl.BlockSpec((tm, A_WIDTH), tokc(1)),
                  pl.BlockSpec((tm, A_WIDTH), tokc(2)),
                  pl.BlockSpec((tm, 2 * LANES), tokc(C_LWA // (2 * LANES))),
                  pl.BlockSpec((tm, A_WIDTH), tokc(C_ZA // A_WIDTH)),
                  pl.BlockSpec((tm, B_WIDTH), tokc(C_ZB // B_WIDTH)),
                  pl.BlockSpec((tm, d), tokc(C_GA // d)),
                  pl.BlockSpec((tm, d), tokc(C_GB // d)),
                  const((2, LANES, A_WIDTH)), const((2, 1, A_WIDTH)), const((1, A_WIDTH)), const((1, A_WIDTH)),
                  const((1, A_WIDTH)), const((1, A_WIDTH)), const((1, B_WIDTH)),
                  const((A_WIDTH, LANES)), const((LANES, A_WIDTH)), const((B_WIDTH, LANES)),
                  const((LANES, B_WIDTH)),
                  const((A_WIDTH, d)), const((B_WIDTH, d)), const((d, d)), const((1, d))],
        out_specs=pl.BlockSpec((tm, d), lambda i: (i, 0)),
        out_shape=jax.ShapeDtypeStruct((rows, d), F32),
        compiler_params=_cparams(("parallel",)),
    )(x2d, gate, y_af, y_ar, o_bf, o_br, rkv_c, rkv_c, rkv_c, p, p, p, p, p,
      wts["aa2"], wts["aa0"], wts["a_k_a"], wts["a_r_k"], wts["lnx_g"], wts["lnx_b"], wts["bng"],
      wts["hsum_a"], wts["hexp_a"], wts["hsum_b"], wts["hexp_b"], wts["w_a"], wts["w_b"], wts["w_out"],
      wts["final_g"])


def _head_indicator(width, head):
    m = (jnp.arange(width)[:, None] // head == jnp.arange(LANES)[None, :]).astype(BF16)
    return m, m.T


def _prep_weights(w_in, conv_w, a_w0, a_w2, a_a0, a_a2, a_k_k, a_k_a, a_r_k, a_lnx_g, a_lnx_b, b_gk_w2,
                  b_gk_b, b_norm_g, w_a, w_b, w_out, final_g):
    sizes = (3 * A_WIDTH, A_WIDTH, 2 * A_RANK, 2 * A_RANK, B_KEY, B_KEY, B_WIDTH, B_WIDTH, 2 * B_GATE_RANK,
             D_MODEL, D_MODEL)
    offs = [0]
    for s in sizes:
        offs.append(offs[-1] + s)
    w_bf = w_in.astype(BF16)
    pad = jnp.zeros((D_MODEL, P_PAD - C_LG - 2 * B_GATE_RANK), BF16)
    w_pieces = (w_bf[:, offs[0]:offs[2]],
                w_bf[:, offs[9]:offs[11]],
                w_bf[:, offs[4]:offs[8]],
                jnp.concatenate([w_bf[:, offs[2]:offs[4]], w_bf[:, offs[8]:offs[9]], pad], axis=1))
    assert tuple(w.shape[1] for w in w_pieces) == IN_PIECES

    def dir_pad(w2, rank):
        out = jnp.zeros((2, LANES, w2.shape[-1]), F32)
        for d in range(2):
            out = out.at[d, d * rank:(d + 1) * rank].set(w2[d])
        return out.astype(BF16)

    hsum_a, hexp_a = _head_indicator(A_WIDTH, A_HEAD)
    hsum_b, hexp_b = _head_indicator(B_WIDTH, B_DV)
    return {
        "w_pieces": w_pieces,
        "conv9": conv_w.reshape(9, 3 * A_WIDTH),
        "aw2": dir_pad(a_w2, A_RANK), "aa2": dir_pad(a_a2, A_RANK),
        "aw0": a_w0.reshape(2, 1, A_WIDTH), "aa0": a_a0.reshape(2, 1, A_WIDTH),
        "a_k_k": a_k_k.reshape(1, A_WIDTH), "a_k_a": a_k_a.reshape(1, A_WIDTH),
        "a_r_k": a_r_k.reshape(1, A_WIDTH),
        "lnx_g": a_lnx_g.reshape(1, A_WIDTH), "lnx_b": a_lnx_b.reshape(1, A_WIDTH),
        "gw2": dir_pad(b_gk_w2, B_GATE_RANK), "gb": b_gk_b.reshape(2, 1, B_KEY),
        "bng": jnp.tile(b_norm_g, B_HEADS).reshape(1, B_WIDTH),
        "hsum_a": hsum_a, "hexp_a": hexp_a, "hsum_b": hsum_b, "hexp_b": hexp_b,
        "w_a": w_a.astype(BF16), "w_b": w_b.astype(BF16), "w_out": w_out.astype(BF16),
        "final_g": final_g.reshape(1, D_MODEL),
    }


def _pick_tile(n, cap):
    t = cap
    while n % t:
        t //= 2
    return t


def kernel(x, c, ctx, c_ctx, w_mod, b_mod, norm_g, w_in, conv_w, a_w0, a_w2, a_a0, a_a2, a_k_k, a_k_a,
           a_r_k, a_lnx_g, a_lnx_b, b_gk_w2, b_gk_b, b_norm_g, w_a, w_b, w_out, final_g):
    assert w_in.shape[0] == 1, "single layer"
    bsz, seq_len, d = x.shape
    ctx_len = ctx.shape[1]
    wts = _prep_weights(w_in[0], conv_w[0], a_w0[0], a_w2[0], a_a0[0], a_a2[0], a_k_k[0], a_k_a[0], a_r_k[0],
                        a_lnx_g[0], a_lnx_b[0], b_gk_w2[0], b_gk_b[0], b_norm_g[0], w_a[0], w_b[0], w_out[0],
                        final_g)

    cond = jnp.concatenate([c, c_ctx[None], jnp.zeros((8 - bsz - 1, d), F32)], axis=0)
    mod = _modulation(cond, w_mod[0], b_mod[0])
    shift, scale, gate = mod[:, :d], mod[:, d:2 * d], mod[:, 2 * d:]
    as_rows = lambda m, lo, hi: m[lo:hi].reshape(hi - lo, 1, d)

    ctx2d = ctx.reshape(bsz * ctx_len, d)
    p_ctx = _inproj(ctx2d, as_rows(scale, bsz, bsz + 1), as_rows(shift, bsz, bsz + 1), norm_g[0],
                    wts["w_pieces"], bsz * ctx_len, _pick_tile(bsz * ctx_len, 1024))
    conv_ctx = wts["conv9"] * jnp.array([0.0, 1.0, 0.0], F32).repeat(3)[:, None]
    rkv_ctx = _conv(p_ctx, conv_ctx, ctx_len, False, _pick_tile(ctx_len, 512))
    zero_a = jnp.zeros((2, bsz, A_PAIRS, LANES, LANES), F32)
    zero_b = jnp.zeros((2, bsz, B_HEADS, B_DV, B_DK), F32)
    *_, sa_ctx, sb_ctx = _scans(rkv_ctx, p_ctx, wts, zero_a, zero_b, bsz, ctx_len)

    x2d = x.reshape(bsz * seq_len, d)
    p = _inproj(x2d, as_rows(scale, 0, bsz), as_rows(shift, 0, bsz), norm_g[0], wts["w_pieces"], seq_len,
                _pick_tile(seq_len, 1024))
    rkv_c = _conv(p, wts["conv9"], seq_len, True, _pick_tile(seq_len, 512))
    y_af, y_ar, o_bf, o_br, _, _ = _scans(rkv_c, p, wts, sa_ctx, sb_ctx, bsz, seq_len)
    out = _merge(x2d, as_rows(gate, 0, bsz), y_af, y_ar, o_bf, o_br, rkv_c, p, wts, bsz, seq_len,
                 _pick_tile(seq_len, 256))
    return out.reshape(bsz, seq_len, d)
```
